```python
import jax, jax.numpy as jnp
from jax import lax
import numpy as np

D_MODEL = 2048
BATCH = 4
SEQ = 4096
DEPTH = 2
DEC_BATCH = 128
DEC_SEQ = 8
PAST_LEN = 16384
PAGE_SIZE = 128

MLA_HEADS = 8
QK_NOPE = 128
QK_ROPE = 64
V_DIM = 128
Q_LORA = 512
KV_LORA = 256
ROPE_THETA = 10000.0
Q_BLOCK = 128
MLA_SCALE = (QK_NOPE + QK_ROPE) ** -0.5
MLA_WIDTH = MLA_HEADS * V_DIM
CACHE_DIM = KV_LORA + QK_ROPE
HG_HEADS = 8
HG_DK = 128
HG_DV = 128
HG_CHUNK = 64
HG_WIDTH = HG_HEADS * HG_DV
GM_GROUPS = 8
GM_GROUP_DIM = 128
GM_CHUNK = 128
GM_WIDTH = GM_GROUPS * GM_GROUP_DIM
D_FF = 5632
N_BRANCH = 3
EPS = 1e-6
IN_SPLITS = (Q_LORA, KV_LORA, QK_ROPE, HG_HEADS * HG_DK, HG_HEADS * HG_DK, HG_WIDTH, HG_WIDTH, GM_WIDTH, GM_WIDTH, N_BRANCH * D_MODEL)
IN_COLS = sum(IN_SPLITS)

kernel_name = 'hybrid_mla_hgrn2_gmlp_step'


def rms_norm(x, g):
    xf = x.astype(jnp.float32)
    y = xf * lax.rsqrt(jnp.mean(xf * xf, axis=-1, keepdims=True) + EPS)
    return (y * g.astype(jnp.float32)).astype(x.dtype)


def layer_norm(x, g, b):
    xf = x.astype(jnp.float32)
    mu = jnp.mean(xf, axis=-1, keepdims=True)
    var = jnp.mean(jnp.square(xf - mu), axis=-1, keepdims=True)
    y = (xf - mu) * lax.rsqrt(var + EPS) * g.astype(jnp.float32) + b.astype(jnp.float32)
    return y.astype(x.dtype)


def rope(x, pos):
    half = x.shape[-1] // 2
    inv = ROPE_THETA ** (-jnp.arange(half, dtype=jnp.float32) / half)
    ang = pos.astype(jnp.float32)[:, None] * inv[None, :]
    ang = ang.reshape(ang.shape[:1] + (1,) * (x.ndim - 3) + ang.shape[1:])
    cos, sin = jnp.cos(ang).astype(x.dtype), jnp.sin(ang).astype(x.dtype)
    x1, x2 = x[..., :half], x[..., half:]
    return jnp.concatenate([x1 * cos - x2 * sin, x1 * sin + x2 * cos], axis=-1)


def swiglu(x, w_gate, w_up, w_down):
    return (jax.nn.silu(x @ w_gate) * (x @ w_up)) @ w_down


def split_in(h):
    idx = [int(i) for i in np.cumsum(IN_SPLITS)[:-1]]
    return jnp.split(h, idx, axis=-1)


def mla_query(c_q, q_norm, w_uq, pos):
    q = rms_norm(c_q, q_norm) @ w_uq
    q = q.reshape(q.shape[:-1] + (MLA_HEADS, QK_NOPE + QK_ROPE))
    return q[..., :QK_NOPE], rope(q[..., QK_NOPE:], pos)


def mla_rows(c_kv, k_r, kv_norm, pos):
    return jnp.concatenate([rms_norm(c_kv, kv_norm), rope(k_r, pos)], axis=-1)


def mla_prompt(q_nope, q_rope, rows, w_ukv):
    B, S = rows.shape[:2]
    c, kr = rows[..., :KV_LORA], rows[..., KV_LORA:]
    kv = (c @ w_ukv).reshape(B, S, MLA_HEADS, QK_NOPE + V_DIM)
    k_nope, v = kv[..., :QK_NOPE], kv[..., QK_NOPE:]
    nb = S // Q_BLOCK
    qn = jnp.moveaxis(q_nope.reshape(B, nb, Q_BLOCK, MLA_HEADS, QK_NOPE), 1, 0)
    qr = jnp.moveaxis(q_rope.reshape(B, nb, Q_BLOCK, MLA_HEADS, QK_ROPE), 1, 0)
    key_pos = jnp.arange(S)

    def block(args):
        qn_b, qr_b, i = args
        s = jnp.einsum('bqhd,bkhd->bhqk', qn_b, k_nope) + jnp.einsum('bqhd,bkd->bhqk', qr_b, kr)
        q_pos = i * Q_BLOCK + jnp.arange(Q_BLOCK)
        s = jnp.where(key_pos[None, :] <= q_pos[:, None], s.astype(jnp.float32) * MLA_SCALE, -jnp.inf)
        p = jax.nn.softmax(s, axis=-1).astype(v.dtype)
        return jnp.einsum('bhqk,bkhd->bqhd', p, v)

    o = lax.map(block, (qn, qr, jnp.arange(nb)))
    return jnp.moveaxis(o, 0, 1).reshape(B, S, MLA_WIDTH)


def mla_sample(q_nope, q_rope, new_rows, cache_l, page_table, w_ukv):
    w = w_ukv.reshape(KV_LORA, MLA_HEADS, QK_NOPE + V_DIM)
    w_uk, w_uv = w[..., :QK_NOPE], w[..., QK_NOPE:]
    q_lat = jnp.einsum('bthd,chd->bthc', q_nope, w_uk)
    T = new_rows.shape[1]
    past = page_table.shape[1] * PAGE_SIZE
    key_idx = jnp.arange(past + T)
    mask = key_idx[None, :] <= (past + jnp.arange(T))[:, None]

    def one(args):
        ql, qr, rows_new, pages = args
        rows = jnp.concatenate([cache_l[pages].reshape(past, CACHE_DIM).astype(rows_new.dtype), rows_new], axis=0)
        c, kr = rows[:, :KV_LORA], rows[:, KV_LORA:]
        s = jnp.einsum('thc,kc->htk', ql, c) + jnp.einsum('thd,kd->htk', qr, kr)
        s = jnp.where(mask[None], s.astype(jnp.float32) * MLA_SCALE, -jnp.inf)
        p = jax.nn.softmax(s, axis=-1).astype(c.dtype)
        return jnp.einsum('htk,kc->thc', p, c)

    o_lat = lax.map(one, (q_lat, q_rope, new_rows, page_table))
    o = jnp.einsum('bthc,chd->bthd', o_lat, w_uv)
    return o.reshape(o.shape[:2] + (MLA_WIDTH,))


def hgrn_inputs(hq, hf, hi, lb):
    B, L = hq.shape[:2]
    shp = (B, L, HG_HEADS, HG_DK)
    f = lb + (1.0 - lb) * jax.nn.sigmoid(hf.astype(jnp.float32))
    q = jax.nn.silu(hq.astype(jnp.float32)).reshape(shp)
    k = (1.0 - f).reshape(shp)
    v = hi.astype(jnp.float32).reshape(B, L, HG_HEADS, HG_DV)
    return q, k, v, jnp.log(f).reshape(shp)


def hgrn_chunk(S0, xs):
    q, k, v, log_f = xs
    C = q.shape[1]
    b = jnp.cumsum(log_f, axis=1)
    causal = jnp.tril(jnp.ones((C, C), dtype=bool))
    diff = b[:, :, None] - b[:, None, :]
    decay = jnp.exp(jnp.where(causal[None, :, :, None, None], diff, -jnp.inf))
    attn = jnp.einsum('bthd,btshd,bshd->bhts', q, decay, k)
    o = jnp.einsum('bhts,bshv->bthv', attn, v) + jnp.einsum('bthd,bhdv->bthv', q * jnp.exp(b), S0)
    bl = b[:, -1]
    S = jnp.exp(bl)[..., None] * S0 + jnp.einsum('bshd,bshv->bhdv', k * jnp.exp(bl[:, None] - b), v)
    return S, o


def hgrn_scan(S0, q, k, v, log_f):
    B, L = q.shape[:2]
    C = min(L, HG_CHUNK)
    nc = L // C

    def chunks(a):
        return jnp.moveaxis(a.reshape((B, nc, C) + a.shape[2:]), 1, 0)

    S, o = lax.scan(hgrn_chunk, S0, (chunks(q), chunks(k), chunks(v), chunks(log_f)))
    return S, jnp.moveaxis(o, 0, 1).reshape(B, L, HG_HEADS, HG_DV)


def hgrn_out(o, hg, gain):
    B, L = o.shape[:2]
    o = rms_norm(o, gain).reshape(B, L, HG_WIDTH)
    return (o * jax.nn.silu(hg.astype(jnp.float32))).astype(hg.dtype)


def gmlp_norm(gv, ln_g, ln_b):
    return layer_norm(jax.nn.gelu(gv), ln_g, ln_b)


def gmlp_mix(gu, vn, w_s, b_s):
    B, L = vn.shape[:2]
    C = min(L, GM_CHUNK)
    nc = L // C
    w = jnp.tril(w_s[:, :C, :C]).astype(vn.dtype)
    vg = vn.reshape(B, nc, C, GM_GROUPS, GM_GROUP_DIM)
    mixed = jnp.einsum('gts,bnsgd->bntgd', w, vg) + b_s[:, :C].T[None, None, :, :, None].astype(vn.dtype)
    return jax.nn.gelu(gu) * mixed.reshape(B, L, GM_WIDTH)


def merge(gate_in, a, b, c, lw):
    g = jax.nn.sigmoid(gate_in.astype(jnp.float32)).astype(gate_in.dtype)
    g = g.reshape(g.shape[:-1] + (N_BRANCH, D_MODEL))
    merged = (g[..., 0, :] * (a @ lw['w_br_mla']) + g[..., 1, :] * (b @ lw['w_br_hgrn'])
              + g[..., 2, :] * (c @ lw['w_br_gmlp']))
    return merged @ lw['w_out']


def token_mixer(h, pos, lw, lb, cache_l=None, state_l=None, page_table=None):
    c_q, c_kv, k_r, hq, hf, hi, hg, gu, gv, gate = split_in(h @ lw['w_in'])
    q_nope, q_rope = mla_query(c_q, lw['mla_q_norm'], lw['w_uq'], pos)
    rows = mla_rows(c_kv, k_r, lw['mla_kv_norm'], pos)
    q, k, v, log_f = hgrn_inputs(hq, hf, hi, lb)
    vn = gmlp_norm(gv, lw['gmlp_ln_g'], lw['gmlp_ln_b'])
    if cache_l is None:
        a = mla_prompt(q_nope, q_rope, rows, lw['w_ukv'])
        S0 = jnp.zeros((h.shape[0], HG_HEADS, HG_DK, HG_DV), jnp.float32)
    else:
        a = mla_sample(q_nope, q_rope, rows, cache_l, page_table, lw['w_ukv'])
        S0 = state_l.astype(jnp.float32)
    S_new, o = hgrn_scan(S0, q, k, v, log_f)
    b = hgrn_out(o, hg, lw['hgrn_out_norm'])
    c = gmlp_mix(gu, vn, lw['gmlp_w_s'], lw['gmlp_b_s'])
    return merge(gate, a, b, c, lw), rows, S_new, vn


def trunk_layer(x, pos, lw, lb, cache_l=None, state_l=None, page_table=None):
    ng = lw['norm_gains']
    h = swiglu(rms_norm(x, ng[0]), lw['w_ffn1_gate'], lw['w_ffn1_up'], lw['w_ffn1_down'])
    x = x + 0.5 * rms_norm(h, ng[1])
    m, rows, S_new, vn = token_mixer(rms_norm(x, ng[2]), pos, lw, lb, cache_l, state_l, page_table)
    x = x + rms_norm(m, ng[3])
    h = swiglu(rms_norm(x, ng[4]), lw['w_ffn2_gate'], lw['w_ffn2_up'], lw['w_ffn2_down'])
    x = x + 0.5 * rms_norm(h, ng[5])
    return x, rows, S_new, vn


def setup_inputs(seed: int = 0) -> dict:
    key = jax.random.key(seed)
    ks = iter(jax.random.split(key, 40))
    n_pages = PAST_LEN // PAGE_SIZE
    n_pool = (5 * DEC_BATCH * n_pages) // 4
    f32 = jnp.float32

    def nrm(shape, scale=1.0):
        return jax.random.normal(next(ks), shape, f32) * scale

    def w(shape, fan_in):
        return nrm(shape, fan_in ** -0.5)

    def gain(shape):
        return 1.0 + nrm(shape, 0.02)

    page_table = jax.random.permutation(next(ks), n_pool)[:DEC_BATCH * n_pages].reshape(DEC_BATCH, n_pages).astype(jnp.int32)
    return {
        'x_prompt': nrm((BATCH, SEQ, D_MODEL)),
        'x_sample': nrm((DEC_BATCH, DEC_SEQ, D_MODEL)),
        'cache_mla': nrm((DEPTH, n_pool, PAGE_SIZE, CACHE_DIM)),
        'state_hgrn': nrm((DEPTH, DEC_BATCH, HG_HEADS, HG_DK, HG_DV), 0.2),
        'page_table': page_table,
        'norm_gains': gain((DEPTH, 6, D_MODEL)),
        'w_ffn1_gate': w((DEPTH, D_MODEL, D_FF), D_MODEL),
        'w_ffn1_up': w((DEPTH, D_MODEL, D_FF), D_MODEL),
        'w_ffn1_down': w((DEPTH, D_FF, D_MODEL), D_FF),
        'w_ffn2_gate': w((DEPTH, D_MODEL, D_FF), D_MODEL),
        'w_ffn2_up': w((DEPTH, D_MODEL, D_FF), D_MODEL),
        'w_ffn2_down': w((DEPTH, D_FF, D_MODEL), D_FF),
        'w_in': w((DEPTH, D_MODEL, IN_COLS), D_MODEL),
        'mla_q_norm': gain((DEPTH, Q_LORA)),
        'mla_kv_norm': gain((DEPTH, KV_LORA)),
        'w_uq': w((DEPTH, Q_LORA, MLA_HEADS * (QK_NOPE + QK_ROPE)), Q_LORA),
        'w_ukv': w((DEPTH, KV_LORA, MLA_HEADS * (QK_NOPE + V_DIM)), KV_LORA),
        'hgrn_lb_logits': nrm((DEPTH, HG_HEADS * HG_DK)),
        'hgrn_out_norm': gain((DEPTH, HG_HEADS, HG_DV)),
        'gmlp_ln_g': gain((DEPTH, GM_WIDTH)),
        'gmlp_ln_b': nrm((DEPTH, GM_WIDTH), 0.02),
        'gmlp_w_s': w((DEPTH, GM_GROUPS, GM_CHUNK, GM_CHUNK), GM_CHUNK),
        'gmlp_b_s': 1.0 + nrm((DEPTH, GM_GROUPS, GM_CHUNK), 0.02),
        'w_br_mla': w((DEPTH, MLA_WIDTH, D_MODEL), MLA_WIDTH),
        'w_br_hgrn': w((DEPTH, HG_WIDTH, D_MODEL), HG_WIDTH),
        'w_br_gmlp': w((DEPTH, GM_WIDTH, D_MODEL), GM_WIDTH),
        'w_out': w((DEPTH, D_MODEL, D_MODEL), D_MODEL),
    }


def reference(x_prompt, x_sample, cache_mla, state_hgrn, page_table, norm_gains,
              w_ffn1_gate, w_ffn1_up, w_ffn1_down, w_ffn2_gate, w_ffn2_up, w_ffn2_down,
              w_in, mla_q_norm, mla_kv_norm, w_uq, w_ukv, hgrn_lb_logits, hgrn_out_norm,
              gmlp_ln_g, gmlp_ln_b, gmlp_w_s, gmlp_b_s, w_br_mla, w_br_hgrn, w_br_gmlp, w_out):
    lb_w = jax.nn.softmax(hgrn_lb_logits.astype(jnp.float32), axis=0)
    lower_bounds = jnp.cumsum(lb_w, axis=0) - lb_w[0]
    past = page_table.shape[1] * PAGE_SIZE
    pos_p = jnp.arange(x_prompt.shape[1])
    pos_s = past + jnp.arange(x_sample.shape[1])
    x_p, x_s = x_prompt, x_sample
    rows_p_all, rows_s_all, S_p_all, S_s_all, v_s_all = [], [], [], [], []
    for l in range(DEPTH):
        lw = dict(norm_gains=norm_gains[l], w_ffn1_gate=w_ffn1_gate[l], w_ffn1_up=w_ffn1_up[l],
                  w_ffn1_down=w_ffn1_down[l], w_ffn2_gate=w_ffn2_gate[l], w_ffn2_up=w_ffn2_up[l],
                  w_ffn2_down=w_ffn2_down[l], w_in=w_in[l], mla_q_norm=mla_q_norm[l],
                  mla_kv_norm=mla_kv_norm[l], w_uq=w_uq[l], w_ukv=w_ukv[l],
                  hgrn_out_norm=hgrn_out_norm[l], gmlp_ln_g=gmlp_ln_g[l], gmlp_ln_b=gmlp_ln_b[l],
                  gmlp_w_s=gmlp_w_s[l], gmlp_b_s=gmlp_b_s[l], w_br_mla=w_br_mla[l],
                  w_br_hgrn=w_br_hgrn[l], w_br_gmlp=w_br_gmlp[l], w_out=w_out[l])
        x_p, rows_p, S_p, _ = trunk_layer(x_p, pos_p, lw, lower_bounds[l])
        x_s, rows_s, S_s, v_s = trunk_layer(x_s, pos_s, lw, lower_bounds[l], cache_mla[l], state_hgrn[l], page_table)
        rows_p_all.append(rows_p)
        rows_s_all.append(rows_s)
        S_p_all.append(S_p)
        S_s_all.append(S_s)
        v_s_all.append(v_s)
    new_mla_prompt = jnp.stack(rows_p_all)
    new_mla_sample = jnp.stack(rows_s_all)
    new_hgrn_prompt = jnp.stack(S_p_all)
    new_hgrn_sample = jnp.stack(S_s_all)
    new_gmlp_v_sample = jnp.stack(v_s_all)
    return (x_p, x_s, new_mla_prompt, new_mla_sample, new_hgrn_prompt, new_hgrn_sample, new_gmlp_v_sample)
```

```python
import functools

import jax
import jax.numpy as jnp
import numpy as np
from jax import lax
from jax.experimental import pallas as pl
from jax.experimental.pallas import tpu as pltpu

D_MODEL = 2048
DEPTH = 2
PAGE_SIZE = 128
MLA_HEADS = 8
QK_NOPE = 128
QK_ROPE = 64
V_DIM = 128
Q_LORA = 512
KV_LORA = 256
ROPE_THETA = 10000.0
MLA_SCALE = (QK_NOPE + QK_ROPE) ** -0.5
MLA_WIDTH = MLA_HEADS * V_DIM
CACHE_DIM = KV_LORA + QK_ROPE
HG_HEADS = 8
HG_DK = 128
HG_DV = 128
HG_WIDTH = HG_HEADS * HG_DV
GM_GROUPS = 8
GM_GROUP_DIM = 128
GM_CHUNK = 128
GM_WIDTH = GM_GROUPS * GM_GROUP_DIM
D_FF = 5632
N_BRANCH = 3
EPS = 1e-6

LANES = 128
Q_HEAD_PAD = 2 * LANES
VMEM_CAP_BYTES = 60000 * 1024
HGRN_SUB = 16

F32 = jnp.float32
BF16 = jnp.bfloat16


def _vmem_limit(block_bytes, temp_bytes=0):
    est = 2 * sum(block_bytes) + temp_bytes + (4 << 20)
    return int(min(max(est, 16 << 20), VMEM_CAP_BYTES))


def _nbytes(shape, dtype):
    return int(np.prod(shape)) * jnp.dtype(dtype).itemsize


def _rms(y, g):
    return y * lax.rsqrt(jnp.mean(y * y, axis=-1, keepdims=True) + EPS) * g


def _rmsnorm_kernel(x_ref, g_ref, o_ref):
    o_ref[...] = _rms(x_ref[...], g_ref[...]).astype(o_ref.dtype)


def rmsnorm_cast(x, g, *, tm):
    m, d = x.shape
    return pl.pallas_call(
        _rmsnorm_kernel,
        grid=(m // tm,),
        in_specs=[pl.BlockSpec((tm, d), lambda i: (i, 0)), pl.BlockSpec((1, d), lambda i: (0, 0))],
        out_specs=pl.BlockSpec((tm, d), lambda i: (i, 0)),
        out_shape=jax.ShapeDtypeStruct((m, d), BF16),
        compiler_params=pltpu.CompilerParams(
            dimension_semantics=("arbitrary",),
            vmem_limit_bytes=_vmem_limit([_nbytes((tm, d), F32), _nbytes((tm, d), BF16)], _nbytes((tm, d), F32))),
        name="rmsnorm_cast",
    )(x, g.reshape(1, d))


def _mm_kernel(*refs, n_w, n_row, n_col, n_out, epi):
    x_ref = refs[0]
    w_refs = refs[1:1 + n_w]
    row_refs = refs[1 + n_w:1 + n_w + n_row]
    col_refs = refs[1 + n_w + n_row:1 + n_w + n_row + n_col]
    out_refs = refs[len(refs) - n_out:]
    x = x_ref[...].astype(BF16)
    accs = [jnp.dot(x, w[...], preferred_element_type=F32) for w in w_refs]
    outs = epi(accs, [r[...] for r in row_refs], [c[...] for c in col_refs])
    for o_ref, o in zip(out_refs, outs):
        o_ref[...] = o.astype(o_ref.dtype)


def mm(x, ws, epi, outs, *, tm, tn, rows=None, row_aux=(), col_aux=(), name):
    m = rows if rows is not None else x.shape[0]
    k = x.shape[1]
    n = ws[0].shape[1]
    nt = n // tn
    assert m % tm == 0 and n % tn == 0
    in_specs = [pl.BlockSpec((tm, k), lambda j, i: (i, 0))]
    in_specs += [pl.BlockSpec((k, tn), lambda j, i: (0, j)) for _ in ws]
    in_specs += [pl.BlockSpec((tm, a.shape[1]), lambda j, i: (i, 0)) for a in row_aux]
    in_specs += [pl.BlockSpec((1, a.shape[1] // nt), lambda j, i: (0, j)) for a in col_aux]
    out_specs = [pl.BlockSpec((tm, w), lambda j, i: (i, j)) for w, _ in outs]
    out_shape = [jax.ShapeDtypeStruct((m, w * nt), dt) for w, dt in outs]
    blocks = [_nbytes((tm, k), x.dtype)] + [_nbytes((k, tn), BF16)] * len(ws)
    blocks += [_nbytes((tm, a.shape[1]), a.dtype) for a in row_aux]
    blocks += [_nbytes((tm, w), dt) for w, dt in outs]
    res = pl.pallas_call(
        functools.partial(_mm_kernel, n_w=len(ws), n_row=len(row_aux), n_col=len(col_aux), n_out=len(outs), epi=epi),
        grid=(nt, m // tm),
        in_specs=in_specs,
        out_specs=out_specs,
        out_shape=out_shape,
        compiler_params=pltpu.CompilerParams(
            dimension_semantics=("arbitrary", "arbitrary"),
            vmem_limit_bytes=_vmem_limit(blocks, 3 * len(ws) * _nbytes((tm, tn), F32))),
        name=name,
    )(x, *ws, *row_aux, *col_aux)
    return res


def _mm_res_kernel(*refs, alpha, nk, has_next):
    if has_next:
        h_ref, w_ref, x_ref, gp_ref, gn_ref, xo_ref, no_ref, acc_ref = refs
    else:
        h_ref, w_ref, x_ref, gp_ref, xo_ref, acc_ref = refs
    k = pl.program_id(1)
    part = jnp.dot(h_ref[...], w_ref[...], preferred_element_type=F32)

    @pl.when(k == 0)
    def _():
        acc_ref[...] = part

    @pl.when(k > 0)
    def _():
        acc_ref[...] += part

    @pl.when(k == nk - 1)
    def _():
        xo = x_ref[...] + alpha * _rms(acc_ref[...], gp_ref[...])
        xo_ref[...] = xo
        if has_next:
            no_ref[...] = _rms(xo, gn_ref[...]).astype(no_ref.dtype)


def mm_res(h, w, x, g_post, g_next, *, alpha, tm, tk, name):
    m, kdim = h.shape
    d = w.shape[1]
    nk = kdim // tk
    assert m % tm == 0 and kdim % tk == 0
    has_next = g_next is not None
    in_specs = [
        pl.BlockSpec((tm, tk), lambda i, k: (i, k)),
        pl.BlockSpec((tk, d), lambda i, k: (k, 0)),
        pl.BlockSpec((tm, d), lambda i, k: (i, 0)),
        pl.BlockSpec((1, d), lambda i, k: (0, 0)),
    ]
    args = [h, w, x, g_post.reshape(1, d)]
    out_specs = [pl.BlockSpec((tm, d), lambda i, k: (i, 0))]
    out_shape = [jax.ShapeDtypeStruct((m, d), F32)]
    blocks = [_nbytes((tm, tk), BF16), _nbytes((tk, d), BF16), 2 * _nbytes((tm, d), F32)]
    if has_next:
        in_specs.append(pl.BlockSpec((1, d), lambda i, k: (0, 0)))
        args.append(g_next.reshape(1, d))
        out_specs.append(pl.BlockSpec((tm, d), lambda i, k: (i, 0)))
        out_shape.append(jax.ShapeDtypeStruct((m, d), BF16))
        blocks.append(_nbytes((tm, d), BF16))
    res = pl.pallas_call(
        functools.partial(_mm_res_kernel, alpha=alpha, nk=nk, has_next=has_next),
        grid=(m // tm, nk),
        in_specs=in_specs,
        out_specs=out_specs,
        out_shape=out_shape,
        scratch_shapes=[pltpu.VMEM((tm, d), F32)],
        compiler_params=pltpu.CompilerParams(
            dimension_semantics=("arbitrary", "arbitrary"),
            vmem_limit_bytes=_vmem_limit(blocks, 3 * _nbytes((tm, d), F32))),
        name=name,
    )(*args)
    return (res[0], res[1]) if has_next else (res[0], None)


def _merge_kernel(a_ref, b_ref, c_ref, g_ref, wa_ref, wb_ref, wc_ref, o_ref, *, d):
    g = g_ref[...].astype(F32)
    acc = g[:, :d] * jnp.dot(a_ref[...], wa_ref[...], preferred_element_type=F32)
    acc += g[:, d:2 * d] * jnp.dot(b_ref[...], wb_ref[...], preferred_element_type=F32)
    acc += g[:, 2 * d:] * jnp.dot(c_ref[...], wc_ref[...], preferred_element_type=F32)
    o_ref[...] = acc.astype(o_ref.dtype)


def merge_branches(a, b, c, g, wa, wb, wc, *, tm):
    m, kw = a.shape
    d = wa.shape[1]
    row = lambda w: pl.BlockSpec((tm, w), lambda i: (i, 0))
    full = pl.BlockSpec((kw, d), lambda i: (0, 0))
    blocks = [3 * _nbytes((tm, kw), BF16), _nbytes((tm, 3 * d), BF16), 3 * _nbytes((kw, d), BF16), _nbytes((tm, d), BF16)]
    return pl.pallas_call(
        functools.partial(_merge_kernel, d=d),
        grid=(m // tm,),
        in_specs=[row(kw), row(kw), row(kw), row(3 * d), full, full, full],
        out_specs=row(d),
        out_shape=jax.ShapeDtypeStruct((m, d), BF16),
        compiler_params=pltpu.CompilerParams(
            dimension_semantics=("arbitrary",),
            vmem_limit_bytes=_vmem_limit(blocks, 4 * _nbytes((tm, d), F32))),
        name="merge_branches",
    )(a, b, c, g, wa, wb, wc)


def _flash_kernel(q_ref, k_ref, v_ref, o_ref, *, tq, scale):
    qi = pl.program_id(2)
    q = q_ref[...]

    def step(kb, carry, masked):
        m_prev, l_prev, acc = carry
        start = pl.multiple_of(kb * tq, tq)
        ks = k_ref[pl.ds(start, tq), :]
        vs = v_ref[pl.ds(start, tq), :]
        s = lax.dot_general(q, ks, (((1,), (1,)), ((), ())), preferred_element_type=F32) * scale
        if masked:
            r = lax.broadcasted_iota(jnp.int32, (tq, tq), 0)
            c = lax.broadcasted_iota(jnp.int32, (tq, tq), 1)
            s = jnp.where(c <= r, s, -jnp.inf)
        m_new = jnp.maximum(m_prev, jnp.max(s, axis=-1, keepdims=True))
        a = jnp.exp(m_prev - m_new)
        p = jnp.exp(s - m_new)
        l_new = a * l_prev + jnp.sum(p, axis=-1, keepdims=True)
        acc = a * acc + jnp.dot(p.astype(BF16), vs, preferred_element_type=F32)
        return m_new, l_new, acc

    init = (jnp.full((tq, 1), -jnp.inf, F32), jnp.zeros((tq, 1), F32), jnp.zeros((tq, v_ref.shape[1]), F32))
    carry = lax.fori_loop(0, qi, lambda kb, c: step(kb, c, False), init)
    _, l_fin, acc = step(qi, carry, True)
    o_ref[...] = (acc / l_fin).astype(o_ref.dtype)


def mla_prompt_attention(q, kfull, v, *, batch, seq, tq):
    nq = seq // tq
    blocks = [_nbytes((tq, Q_HEAD_PAD), BF16), _nbytes((seq, Q_HEAD_PAD), BF16), _nbytes((seq, V_DIM), BF16),
              _nbytes((tq, V_DIM), BF16)]
    return pl.pallas_call(
        functools.partial(_flash_kernel, tq=tq, scale=MLA_SCALE),
        grid=(batch, MLA_HEADS, nq),
        in_specs=[
            pl.BlockSpec((tq, Q_HEAD_PAD), lambda b, h, i: (b * nq + i, h)),
            pl.BlockSpec((seq, Q_HEAD_PAD), lambda b, h, i: (b, h)),
            pl.BlockSpec((seq, V_DIM), lambda b, h, i: (b, h)),
        ],
        out_specs=pl.BlockSpec((tq, V_DIM), lambda b, h, i: (b * nq + i, h)),
        out_shape=jax.ShapeDtypeStruct((batch * seq, MLA_WIDTH), BF16),
        compiler_params=pltpu.CompilerParams(
            dimension_semantics=("arbitrary", "arbitrary", "arbitrary"),
            vmem_limit_bytes=_vmem_limit(blocks, 6 * _nbytes((tq, tq), F32))),
        name="mla_prompt_attention",
    )(q, kfull, v)


def _dec_kernel(pt_ref, q_ref, new_ref, cache_ref, o_ref, buf, sem, *, layer, group, n_groups, t_new, scale):
    b = pl.program_id(0)
    rows_q = q_ref.shape[1]

    def page_copy(g, slot, p):
        page = pt_ref[b, g * group + p]
        return pltpu.make_async_copy(cache_ref.at[layer, page], buf.at[slot, p], sem.at[slot])

    def start(g, slot):
        for p in range(group):
            page_copy(g, slot, p).start()

    def wait(g, slot):
        for p in range(group):
            page_copy(g, slot, p).wait()

    def fold(carry, keys, s):
        m_prev, l_prev, acc = carry
        m_new = jnp.maximum(m_prev, jnp.max(s, axis=-1, keepdims=True))
        a = jnp.exp(m_prev - m_new)
        p = jnp.exp(s - m_new)
        l_new = a * l_prev + jnp.sum(p, axis=-1, keepdims=True)
        acc = a * acc + jnp.dot(p.astype(BF16), keys[:, :KV_LORA], preferred_element_type=F32)
        return m_new, l_new, acc

    q = q_ref[0].astype(BF16)
    carry = (jnp.full((rows_q, 1), -jnp.inf, F32), jnp.zeros((rows_q, 1), F32), jnp.zeros((rows_q, KV_LORA), F32))
    start(0, 0)
    for g in range(n_groups):
        slot = g % 2
        if g + 1 < n_groups:
            start(g + 1, 1 - slot)
        wait(g, slot)
        keys = buf[slot].reshape(group * PAGE_SIZE, CACHE_DIM).astype(BF16)
        s = lax.dot_general(q, keys, (((1,), (1,)), ((), ())), preferred_element_type=F32) * scale
        carry = fold(carry, keys, s)
    keys = new_ref[0].astype(BF16)
    s = lax.dot_general(q, keys, (((1,), (1,)), ((), ())), preferred_element_type=F32) * scale
    t_of_row = lax.broadcasted_iota(jnp.int32, (rows_q, t_new), 0) % t_new
    j = lax.broadcasted_iota(jnp.int32, (rows_q, t_new), 1)
    s = jnp.where(j <= t_of_row, s, -jnp.inf)
    _, l_fin, acc = fold(carry, keys, s)
    o_ref[0] = acc / l_fin


def mla_sample_attention(qcat, rows_new, cache, page_table, *, layer, group):
    nb, rows_q, _ = qcat.shape
    t_new = rows_new.shape[1]
    n_pages = page_table.shape[1]
    assert n_pages % group == 0
    n_groups = n_pages // group
    grid_spec = pltpu.PrefetchScalarGridSpec(
        num_scalar_prefetch=1,
        grid=(nb,),
        in_specs=[
            pl.BlockSpec((1, rows_q, CACHE_DIM), lambda b, pt: (b, 0, 0)),
            pl.BlockSpec((1, t_new, CACHE_DIM), lambda b, pt: (b, 0, 0)),
            pl.BlockSpec(memory_space=pl.ANY),
        ],
        out_specs=pl.BlockSpec((1, rows_q, KV_LORA), lambda b, pt: (b, 0, 0)),
        scratch_shapes=[
            pltpu.VMEM((2, group, PAGE_SIZE, CACHE_DIM), F32),
            pltpu.SemaphoreType.DMA((2,)),
        ],
    )
    buf_bytes = 2 * group * PAGE_SIZE * 3 * LANES * 4
    return pl.pallas_call(
        functools.partial(_dec_kernel, layer=layer, group=group, n_groups=n_groups, t_new=t_new, scale=MLA_SCALE),
        grid_spec=grid_spec,
        out_shape=jax.ShapeDtypeStruct((nb, rows_q, KV_LORA), F32),
        compiler_params=pltpu.CompilerParams(
            dimension_semantics=("arbitrary",),
            vmem_limit_bytes=_vmem_limit([buf_bytes // 2], buf_bytes + 8 * _nbytes((rows_q, group * PAGE_SIZE), F32))),
        name="mla_sample_attention",
    )(page_table, qcat, rows_new, cache)


def _qlat_kernel(q_ref, w_ref, o_ref, *, nb, t):
    q = q_ref[...]
    lat = jnp.dot(q[:, :QK_NOPE], w_ref[0], preferred_element_type=F32)
    cat = jnp.concatenate([lat, q[:, QK_NOPE:QK_NOPE + QK_ROPE].astype(F32)], axis=-1)
    o_ref[...] = cat.reshape(nb, 1, t, CACHE_DIM).astype(o_ref.dtype)


def sample_query_latent(q, w_lat, *, row0, nb, t):
    m_s = nb * t
    blk0 = row0 // m_s
    return pl.pallas_call(
        functools.partial(_qlat_kernel, nb=nb, t=t),
        grid=(MLA_HEADS,),
        in_specs=[
            pl.BlockSpec((m_s, Q_HEAD_PAD), lambda h: (blk0, h)),
            pl.BlockSpec((1, QK_NOPE, KV_LORA), lambda h: (h, 0, 0)),
        ],
        out_specs=pl.BlockSpec((nb, 1, t, CACHE_DIM), lambda h: (0, h, 0, 0)),
        out_shape=jax.ShapeDtypeStruct((nb, MLA_HEADS, t, CACHE_DIM), F32),
        compiler_params=pltpu.CompilerParams(dimension_semantics=("arbitrary",)),
        name="sample_query_latent",
    )(q, w_lat)


def _oproj_kernel(o_ref, w_ref, a_ref, *, nb, t):
    o = o_ref[...].reshape(nb * t, KV_LORA).astype(BF16)
    a_ref[...] = jnp.dot(o, w_ref[0], preferred_element_type=F32).astype(a_ref.dtype)


def sample_value_proj(o_lat, w_uv, *, nb, t):
    return pl.pallas_call(
        functools.partial(_oproj_kernel, nb=nb, t=t),
        grid=(MLA_HEADS,),
        in_specs=[
            pl.BlockSpec((nb, 1, t, KV_LORA), lambda h: (0, h, 0, 0)),
            pl.BlockSpec((1, KV_LORA, V_DIM), lambda h: (h, 0, 0)),
        ],
        out_specs=pl.BlockSpec((nb * t, V_DIM), lambda h: (0, h)),
        out_shape=jax.ShapeDtypeStruct((nb * t, MLA_WIDTH), BF16),
        compiler_params=pltpu.CompilerParams(dimension_semantics=("arbitrary",)),
        name="sample_value_proj",
    )(o_lat, w_uv)


def _hgrn_kernel(*refs, layer, sub, n_sub, nseq, has_init):
    if has_init:
        lbl_ref, hq_ref, hf_ref, hi_ref, hg_ref, gain_ref, s0_ref, b_ref, sout_ref, st_ref = refs
    else:
        lbl_ref, hq_ref, hf_ref, hi_ref, hg_ref, gain_ref, b_ref, sout_ref, st_ref = refs
    rows = nseq * n_sub * sub
    logits = lbl_ref[...]
    e = jnp.exp(logits - jnp.max(logits, axis=0, keepdims=True))
    wsm = e / jnp.sum(e, axis=0, keepdims=True)
    lb = jnp.sum(wsm[:layer + 1], axis=0, keepdims=True) - wsm[0:1]

    f = lb + (1.0 - lb) * jax.nn.sigmoid(hf_ref[...])
    logf = jnp.log(f)
    kk = 1.0 - f
    q = jax.nn.silu(hq_ref[...])
    v = hi_ref[...]
    in_blk = jnp.bitwise_and(lax.broadcasted_iota(jnp.int32, (rows, HG_DK), 0), sub - 1)
    bcum = logf
    shift = 1
    while shift < sub:
        bcum = bcum + jnp.where(in_blk >= shift, pltpu.roll(bcum, shift, 0), 0.0)
        shift *= 2
    qt = q * jnp.exp(bcum)
    tri = lax.broadcasted_iota(jnp.int32, (sub, sub), 1) <= lax.broadcasted_iota(jnp.int32, (sub, sub), 0)

    if not has_init:
        @pl.when(pl.program_id(2) == 0)
        def _():
            st_ref[...] = jnp.zeros_like(st_ref)

    outs = []
    for s in range(nseq):
        st = s0_ref[s, 0].T if has_init else st_ref[...]
        for j in range(n_sub):
            sl = slice((s * n_sub + j) * sub, (s * n_sub + j + 1) * sub)
            bj, qj, kj, vj = bcum[sl], q[sl], kk[sl], v[sl].astype(BF16)
            o_inter = lax.dot_general(qt[sl].astype(BF16), st.astype(BF16), (((1,), (1,)), ((), ())),
                                      preferred_element_type=F32)
            diff = jnp.minimum(bj[:, None, :] - bj[None, :, :], 0.0)
            att = jnp.sum(qj[:, None, :] * kj[None, :, :] * jnp.exp(diff), axis=-1)
            att = jnp.where(tri, att, 0.0)
            o_intra = jnp.dot(att.astype(BF16), vj, preferred_element_type=F32)
            bl = bj[sub - 1:sub, :]
            kt = (kj * jnp.exp(bl - bj)).astype(BF16)
            upd = lax.dot_general(vj, kt, (((0,), (0,)), ((), ())), preferred_element_type=F32)
            st = st * jnp.exp(bl) + upd
            outs.append(o_intra + o_inter)
        if has_init:
            sout_ref[s, 0] = st.T
        else:
            st_ref[...] = st

            @pl.when(pl.program_id(2) == pl.num_programs(2) - 1)
            def _():
                sout_ref[0, 0] = st.T
    o = jnp.concatenate(outs, axis=0)
    b_ref[...] = (_rms(o, gain_ref[...]) * jax.nn.silu(hg_ref[...])).astype(b_ref.dtype)


def hgrn_mixer(hproj, lb_logits, gain, state0, *, layer, row0, nseq_total, seq_len, blk_len, nseq, sub):
    has_init = state0 is not None
    n_sub = blk_len // sub
    nblk = seq_len // blk_len
    rows = nseq * blk_len
    assert row0 % rows == 0 and nseq_total % nseq == 0 and (nseq == 1 or nblk == 1)
    r0 = row0 // rows
    h_ = HG_HEADS
    col = lambda p: pl.BlockSpec((rows, HG_DK), lambda s, h, c: (r0 + s * nblk + c, p * h_ + h))
    in_specs = [
        pl.BlockSpec((DEPTH, HG_DK), lambda s, h, c: (0, h)),
        col(0), col(1), col(2), col(3),
        pl.BlockSpec((1, HG_DV), lambda s, h, c: (0, h)),
    ]
    args = [lb_logits, hproj, hproj, hproj, hproj, gain.reshape(1, HG_WIDTH)]
    if has_init:
        in_specs.append(pl.BlockSpec((nseq, 1, HG_DK, HG_DV), lambda s, h, c: (s, h, 0, 0)))
        args.append(state0)
    out_specs = [
        pl.BlockSpec((rows, HG_DV), lambda s, h, c: (s * nblk + c, h)),
        pl.BlockSpec((nseq, 1, HG_DK, HG_DV), lambda s, h, c: (s, h, 0, 0)),
    ]
    out_shape = [
        jax.ShapeDtypeStruct((nseq_total * seq_len, HG_WIDTH), BF16),
        jax.ShapeDtypeStruct((nseq_total, HG_HEADS, HG_DK, HG_DV), F32),
    ]
    return pl.pallas_call(
        functools.partial(_hgrn_kernel, layer=layer, sub=sub, n_sub=n_sub, nseq=nseq, has_init=has_init),
        grid=(nseq_total // nseq, HG_HEADS, nblk),
        in_specs=in_specs,
        out_specs=out_specs,
        out_shape=out_shape,
        scratch_shapes=[pltpu.VMEM((HG_DV, HG_DK), F32)],
        compiler_params=pltpu.CompilerParams(
            dimension_semantics=("arbitrary", "arbitrary", "arbitrary"),
            vmem_limit_bytes=32 << 20),
        name="hgrn_prompt" if not has_init else "hgrn_sample",
    )(*args)


def _gmlp_kernel(gu_ref, vn_ref, w_ref, bias_ref, o_ref, *, n_chunks):
    for c in range(n_chunks):
        rs = slice(c * GM_CHUNK, (c + 1) * GM_CHUNK)
        vn = vn_ref[rs, :].astype(BF16)
        parts = []
        for g in range(GM_GROUPS):
            cs = slice(g * GM_GROUP_DIM, (g + 1) * GM_GROUP_DIM)
            parts.append(jnp.dot(w_ref[0, g], vn[:, cs], preferred_element_type=F32))
        mixed = jnp.concatenate(parts, axis=-1) + bias_ref[0]
        o_ref[rs, :] = (gu_ref[rs, :].astype(F32) * mixed).astype(o_ref.dtype)


def gmlp_mix(gu_act, vn, wmix, bias, *, tm, prompt_rows):
    m = gu_act.shape[0]
    first_sample_tile = prompt_rows // tm
    kind = lambda i: jnp.where(i >= first_sample_tile, 1, 0)
    return pl.pallas_call(
        functools.partial(_gmlp_kernel, n_chunks=tm // GM_CHUNK),
        grid=(m // tm,),
        in_specs=[
            pl.BlockSpec((tm, GM_WIDTH), lambda i: (i, 0)),
            pl.BlockSpec((tm, GM_WIDTH), lambda i: (i, 0)),
            pl.BlockSpec((1, GM_GROUPS, GM_CHUNK, GM_CHUNK), lambda i: (kind(i), 0, 0, 0)),
            pl.BlockSpec((1, GM_CHUNK, GM_WIDTH), lambda i: (kind(i), 0, 0)),
        ],
        out_specs=pl.BlockSpec((tm, GM_WIDTH), lambda i: (i, 0)),
        out_shape=jax.ShapeDtypeStruct((m, GM_WIDTH), BF16),
        compiler_params=pltpu.CompilerParams(dimension_semantics=("arbitrary",), vmem_limit_bytes=32 << 20),
        name="gmlp_mix",
    )(gu_act, vn, wmix, bias)


def _rope_tile(t, cos_t, sin_t):
    return t * cos_t + pltpu.roll(t, QK_ROPE, 1) * sin_t


def _epi_ffn_up(accs, row_aux, col_aux):
    return [jax.nn.silu(accs[0]) * accs[1]]


def _epi_small(accs, row_aux, col_aux):
    y = accs[0]
    cos_t, sin_t = row_aux
    qg, kvg = col_aux
    cqn = _rms(y[:, :Q_LORA], qg)
    ckvn = _rms(y[:, Q_LORA:Q_LORA + KV_LORA], kvg)
    roped = _rope_tile(y[:, Q_LORA + KV_LORA:], cos_t, sin_t)
    return [cqn, jnp.concatenate([ckvn, roped[:, :QK_ROPE]], axis=-1)]


def _epi_identity(accs, row_aux, col_aux):
    return [accs[0]]


def _epi_gelu(accs, row_aux, col_aux):
    return [jax.nn.gelu(accs[0])]


def _epi_gelu_ln(accs, row_aux, col_aux):
    y = jax.nn.gelu(accs[0])
    g, b = col_aux
    mu = jnp.mean(y, axis=-1, keepdims=True)
    var = jnp.mean(jnp.square(y - mu), axis=-1, keepdims=True)
    return [(y - mu) * lax.rsqrt(var + EPS) * g + b]


def _epi_sigmoid(accs, row_aux, col_aux):
    return [jax.nn.sigmoid(accs[0])]


def _epi_q_rope(accs, row_aux, col_aux):
    y = accs[0]
    cos_t, sin_t = row_aux
    parts = []
    for h in range(MLA_HEADS):
        base = h * Q_HEAD_PAD
        parts.append(y[:, base:base + QK_NOPE])
        parts.append(_rope_tile(y[:, base + QK_NOPE:base + Q_HEAD_PAD], cos_t, sin_t))
    return [jnp.concatenate(parts, axis=-1)]


def _epi_kv_split(accs, row_aux, col_aux):
    n_k = MLA_HEADS * Q_HEAD_PAD
    return [accs[0][:, :n_k], accs[0][:, n_k:]]


def _rot_cols(w):
    half = QK_ROPE // 2
    return jnp.concatenate([-w[..., half:], w[..., :half]], axis=-1)


def _prep_layer(l, w_in, w_uq, w_ukv, gmlp_w_s, gmlp_b_s, t_sample):
    wi = w_in[l]
    o_kr = Q_LORA + KV_LORA
    o_h = o_kr + QK_ROPE
    w_small = jnp.concatenate([wi[:, :o_h], _rot_cols(wi[:, o_kr:o_h])], axis=-1).astype(BF16)
    w_hgrn = wi[:, o_h:o_h + 4 * HG_WIDTH].astype(BF16)
    o_g = o_h + 4 * HG_WIDTH
    w_gu = wi[:, o_g:o_g + GM_WIDTH].astype(BF16)
    w_gv = wi[:, o_g + GM_WIDTH:o_g + 2 * GM_WIDTH].astype(BF16)
    w_gate = wi[:, o_g + 2 * GM_WIDTH:].astype(BF16)

    uq = w_uq[l].reshape(Q_LORA, MLA_HEADS, QK_NOPE + QK_ROPE)
    uq_rope = uq[..., QK_NOPE:]
    w_q = jnp.concatenate([uq[..., :QK_NOPE], uq_rope, _rot_cols(uq_rope)], axis=-1)
    w_q = w_q.reshape(Q_LORA, MLA_HEADS * Q_HEAD_PAD).astype(BF16)

    ukv = w_ukv[l].reshape(KV_LORA, MLA_HEADS, QK_NOPE + V_DIM)
    w_uk, w_uv = ukv[..., :QK_NOPE], ukv[..., QK_NOPE:]
    k_top = jnp.concatenate([w_uk, jnp.zeros((KV_LORA, MLA_HEADS, Q_HEAD_PAD - QK_NOPE), F32)], axis=-1)
    eye = jnp.concatenate([jnp.zeros((QK_ROPE, QK_NOPE), F32), jnp.eye(QK_ROPE, dtype=F32),
                           jnp.zeros((QK_ROPE, Q_HEAD_PAD - QK_NOPE - QK_ROPE), F32)], axis=-1)
    k_bot = jnp.broadcast_to(eye[:, None, :], (QK_ROPE, MLA_HEADS, Q_HEAD_PAD))
    w_k = jnp.concatenate([k_top, k_bot], axis=0).reshape(CACHE_DIM, MLA_HEADS * Q_HEAD_PAD).astype(BF16)
    w_v = jnp.concatenate([w_uv, jnp.zeros((QK_ROPE, MLA_HEADS, V_DIM), F32)], axis=0)
    w_v = w_v.reshape(CACHE_DIM, MLA_WIDTH).astype(BF16)
    w_lat = jnp.transpose(w_uk, (1, 2, 0)).astype(BF16)
    w_uvh = jnp.transpose(w_uv, (1, 0, 2)).astype(BF16)

    ws = gmlp_w_s[l]
    w_prompt = jnp.tril(ws)
    small = jnp.tril(ws[:, :t_sample, :t_sample])
    reps = GM_CHUNK // t_sample
    w_sample = jnp.einsum("ab,gts->gatbs", jnp.eye(reps, dtype=F32), small).reshape(GM_GROUPS, GM_CHUNK, GM_CHUNK)
    wmix = jnp.stack([w_prompt, w_sample]).astype(BF16)
    bs = gmlp_b_s[l]
    bias_p = jnp.repeat(bs.T[:, :, None], GM_GROUP_DIM, axis=2).reshape(GM_CHUNK, GM_WIDTH)
    bias_s = jnp.tile(jnp.repeat(bs[:, :t_sample].T[:, :, None], GM_GROUP_DIM, axis=2).reshape(t_sample, GM_WIDTH),
                      (reps, 1))
    bias = jnp.stack([bias_p, bias_s])
    return dict(w_small=w_small, w_hgrn=w_hgrn, w_gu=w_gu, w_gv=w_gv, w_gate=w_gate, w_q=w_q,
                w_kv=jnp.concatenate([w_k, w_v], axis=-1), w_lat=w_lat, w_uvh=w_uvh, wmix=wmix, bias=bias)


def _rope_tables(pos):
    half = QK_ROPE // 2
    inv = ROPE_THETA ** (-jnp.arange(half, dtype=F32) / half)
    ang = pos.astype(F32)[:, None] * inv[None, :]
    pad = jnp.zeros((pos.shape[0], LANES - QK_ROPE), F32)
    cos_t = jnp.concatenate([jnp.cos(ang), jnp.cos(ang), pad], axis=-1)
    sin_t = jnp.concatenate([jnp.sin(ang), jnp.sin(ang), pad], axis=-1)
    return cos_t, sin_t


def kernel(x_prompt, x_sample, cache_mla, state_hgrn, page_table, norm_gains, w_ffn1_gate, w_ffn1_up, w_ffn1_down, w_ffn2_gate, w_ffn2_up, w_ffn2_down, w_in, mla_q_norm, mla_kv_norm, w_uq, w_ukv, hgrn_lb_logits, hgrn_out_norm, gmlp_ln_g, gmlp_ln_b, gmlp_w_s, gmlp_b_s, w_br_mla, w_br_hgrn, w_br_gmlp, w_out):
    batch, seq, d = x_prompt.shape
    nb, t_s, _ = x_sample.shape
    m_p, m_s = batch * seq, nb * t_s
    m = m_p + m_s
    past = page_table.shape[1] * PAGE_SIZE
    tm = 1024
    while m_p % tm or m_s % tm:
        tm //= 2
    th = min(512, tm)
    assert tm >= GM_CHUNK and m_p % m_s == 0

    x = jnp.concatenate([x_prompt.reshape(m_p, d), x_sample.reshape(m_s, d)], axis=0)
    pos = jnp.concatenate([jnp.tile(jnp.arange(seq), batch), jnp.tile(past + jnp.arange(t_s), nb)])
    cos_t, sin_t = _rope_tables(pos)
    bf = lambda w: w.astype(BF16)

    rows_all, s_p_all, s_s_all, vn_all = [], [], [], []
    xn = rmsnorm_cast(x, norm_gains[0, 0], tm=th)
    for l in range(DEPTH):
        ng = norm_gains[l]
        p = _prep_layer(l, w_in, w_uq, w_ukv, gmlp_w_s, gmlp_b_s, t_s)

        hmid = mm(xn, [bf(w_ffn1_gate[l]), bf(w_ffn1_up[l])], _epi_ffn_up, [(512, BF16)], tm=tm, tn=512,
                  name="ffn1_up")[0]
        x, hn = mm_res(hmid, bf(w_ffn1_down[l]), x, ng[1], ng[2], alpha=0.5, tm=th, tk=1408, name="ffn1_down")

        cqn, rows = mm(hn, [p["w_small"]], _epi_small, [(Q_LORA, BF16), (CACHE_DIM, F32)], tm=th,
                       tn=p["w_small"].shape[1], row_aux=(cos_t, sin_t),
                       col_aux=(mla_q_norm[l].reshape(1, -1), mla_kv_norm[l].reshape(1, -1)), name="in_proj_mla")
        hproj = mm(hn, [p["w_hgrn"]], _epi_identity, [(1024, F32)], tm=tm, tn=1024, name="in_proj_hgrn")[0]
        gu_act = mm(hn, [p["w_gu"]], _epi_gelu, [(GM_WIDTH, BF16)], tm=tm, tn=GM_WIDTH, name="in_proj_gu")[0]
        vn = mm(hn, [p["w_gv"]], _epi_gelu_ln, [(GM_WIDTH, F32)], tm=tm, tn=GM_WIDTH,
                col_aux=(gmlp_ln_g[l].reshape(1, -1), gmlp_ln_b[l].reshape(1, -1)), name="in_proj_gv")[0]
        gate = mm(hn, [p["w_gate"]], _epi_sigmoid, [(1024, BF16)], tm=tm, tn=1024, name="in_proj_gate")[0]

        q = mm(cqn, [p["w_q"]], _epi_q_rope, [(MLA_HEADS * Q_HEAD_PAD, BF16)], tm=th, tn=MLA_HEADS * Q_HEAD_PAD,
               row_aux=(cos_t, sin_t), name="mla_q")[0]
        kfull, vv = mm(rows, [p["w_kv"]], _epi_kv_split, [(MLA_HEADS * Q_HEAD_PAD, BF16), (MLA_WIDTH, BF16)],
                       tm=th, tn=p["w_kv"].shape[1], rows=m_p, name="mla_kv")
        a_p = mla_prompt_attention(q, kfull, vv, batch=batch, seq=seq, tq=min(512, seq))

        qcat = sample_query_latent(q, p["w_lat"], row0=m_p, nb=nb, t=t_s)
        o_lat = mla_sample_attention(qcat.reshape(nb, MLA_HEADS * t_s, CACHE_DIM), rows[m_p:].reshape(nb, t_s, CACHE_DIM),
                                     cache_mla, page_table, layer=l, group=32)
        a_s = sample_value_proj(o_lat.reshape(nb, MLA_HEADS, t_s, KV_LORA), p["w_uvh"], nb=nb, t=t_s)
        a = jnp.concatenate([a_p, a_s], axis=0)

        b_p, s_p = hgrn_mixer(hproj, hgrn_lb_logits, hgrn_out_norm[l], None, layer=l, row0=0, nseq_total=batch,
                              seq_len=seq, blk_len=256, nseq=1, sub=HGRN_SUB)
        b_s, s_s = hgrn_mixer(hproj, hgrn_lb_logits, hgrn_out_norm[l], state_hgrn[l], layer=l, row0=m_p,
                              nseq_total=nb, seq_len=t_s, blk_len=t_s, nseq=8, sub=t_s)
        bmix = jnp.concatenate([b_p, b_s], axis=0)

        cmix = gmlp_mix(gu_act, vn, p["wmix"], p["bias"], tm=th, prompt_rows=m_p)

        merged = merge_branches(a, bmix, cmix, gate, bf(w_br_mla[l]), bf(w_br_hgrn[l]), bf(w_br_gmlp[l]), tm=th)
        x, hn4 = mm_res(merged, bf(w_out[l]), x, ng[3], ng[4], alpha=1.0, tm=th, tk=d, name="merge_out")

        hmid = mm(hn4, [bf(w_ffn2_gate[l]), bf(w_ffn2_up[l])], _epi_ffn_up, [(512, BF16)], tm=tm, tn=512,
                  name="ffn2_up")[0]
        g_next = norm_gains[l + 1, 0] if l + 1 < DEPTH else None
        x, xn = mm_res(hmid, bf(w_ffn2_down[l]), x, ng[5], g_next, alpha=0.5, tm=th, tk=1408, name="ffn2_down")

        rows_all.append(rows)
        s_p_all.append(s_p)
        s_s_all.append(s_s)
        vn_all.append(vn[m_p:])

    rows_st = jnp.stack(rows_all)
    return (x[:m_p].reshape(batch, seq, d), x[m_p:].reshape(nb, t_s, d),
            rows_st[:, :m_p].reshape(DEPTH, batch, seq, CACHE_DIM), rows_st[:, m_p:].reshape(DEPTH, nb, t_s, CACHE_DIM),
            jnp.stack(s_p_all), jnp.stack(s_s_all), jnp.stack(vn_all).reshape(DEPTH, nb, t_s, GM_WIDTH))
```

```python
import functools

import jax
import jax.numpy as jnp
import numpy as np
from jax import lax
from jax.experimental import pallas as pl
from jax.experimental.pallas import tpu as pltpu

D_MODEL = 2048
DEPTH = 2
PAGE_SIZE = 128
MLA_HEADS = 8
QK_NOPE = 128
QK_ROPE = 64
V_DIM = 128
Q_LORA = 512
KV_LORA = 256
ROPE_THETA = 10000.0
MLA_SCALE = (QK_NOPE + QK_ROPE) ** -0.5
MLA_WIDTH = MLA_HEADS * V_DIM
CACHE_DIM = KV_LORA + QK_ROPE
HG_HEADS = 8
HG_DK = 128
HG_DV = 128
HG_WIDTH = HG_HEADS * HG_DV
GM_GROUPS = 8
GM_GROUP_DIM = 128
GM_CHUNK = 128
GM_WIDTH = GM_GROUPS * GM_GROUP_DIM
D_FF = 5632
N_BRANCH = 3
EPS = 1e-6

LANES = 128
Q_HEAD_PAD = 2 * LANES
VMEM_CAP_BYTES = 60000 * 1024
HGRN_SUB = 16
HGRN_WIDE = 32
HGRN_SAFE_LOG = 60.0

F32 = jnp.float32
BF16 = jnp.bfloat16


def _vmem_limit(block_bytes, temp_bytes=0):
    est = 2 * sum(block_bytes) + temp_bytes + (4 << 20)
    return int(min(max(est, 16 << 20), VMEM_CAP_BYTES))


def _nbytes(shape, dtype):
    return int(np.prod(shape)) * jnp.dtype(dtype).itemsize


def _rms(y, g):
    return y * lax.rsqrt(jnp.mean(y * y, axis=-1, keepdims=True) + EPS) * g


def _rmsnorm_kernel(x_ref, g_ref, o_ref):
    o_ref[...] = _rms(x_ref[...], g_ref[...]).astype(o_ref.dtype)


def rmsnorm_cast(x, g, *, tm):
    m, d = x.shape
    return pl.pallas_call(
        _rmsnorm_kernel,
        grid=(m // tm,),
        in_specs=[pl.BlockSpec((tm, d), lambda i: (i, 0)), pl.BlockSpec((1, d), lambda i: (0, 0))],
        out_specs=pl.BlockSpec((tm, d), lambda i: (i, 0)),
        out_shape=jax.ShapeDtypeStruct((m, d), BF16),
        compiler_params=pltpu.CompilerParams(
            dimension_semantics=("arbitrary",),
            vmem_limit_bytes=_vmem_limit([_nbytes((tm, d), F32), _nbytes((tm, d), BF16)], _nbytes((tm, d), F32))),
        name="rmsnorm_cast",
    )(x, g.reshape(1, d))


def _mm_kernel(*refs, n_w, n_row, n_col, n_out, epi):
    x_ref = refs[0]
    w_refs = refs[1:1 + n_w]
    row_refs = refs[1 + n_w:1 + n_w + n_row]
    col_refs = refs[1 + n_w + n_row:1 + n_w + n_row + n_col]
    out_refs = refs[len(refs) - n_out:]
    x = x_ref[...].astype(BF16)
    accs = [jnp.dot(x, w[...], preferred_element_type=F32) for w in w_refs]
    outs = epi(accs, [r[...] for r in row_refs], [c[...] for c in col_refs])
    for o_ref, o in zip(out_refs, outs):
        o_ref[...] = o.astype(o_ref.dtype)


def mm(x, ws, epi, outs, *, tm, tn, rows=None, row_aux=(), col_aux=(), name):
    m = rows if rows is not None else x.shape[0]
    k = x.shape[1]
    n = ws[0].shape[1]
    nt = n // tn
    assert m % tm == 0 and n % tn == 0
    in_specs = [pl.BlockSpec((tm, k), lambda j, i: (i, 0))]
    in_specs += [pl.BlockSpec((k, tn), lambda j, i: (0, j)) for _ in ws]
    in_specs += [pl.BlockSpec((tm, a.shape[1]), lambda j, i: (i, 0)) for a in row_aux]
    in_specs += [pl.BlockSpec((1, a.shape[1] // nt), lambda j, i: (0, j)) for a in col_aux]
    out_specs = [pl.BlockSpec((tm, w), lambda j, i: (i, j)) for w, _ in outs]
    out_shape = [jax.ShapeDtypeStruct((m, w * nt), dt) for w, dt in outs]
    blocks = [_nbytes((tm, k), x.dtype)] + [_nbytes((k, tn), BF16)] * len(ws)
    blocks += [_nbytes((tm, a.shape[1]), a.dtype) for a in row_aux]
    blocks += [_nbytes((tm, w), dt) for w, dt in outs]
    res = pl.pallas_call(
        functools.partial(_mm_kernel, n_w=len(ws), n_row=len(row_aux), n_col=len(col_aux), n_out=len(outs), epi=epi),
        grid=(nt, m // tm),
        in_specs=in_specs,
        out_specs=out_specs,
        out_shape=out_shape,
        compiler_params=pltpu.CompilerParams(
            dimension_semantics=("arbitrary", "arbitrary"),
            vmem_limit_bytes=_vmem_limit(blocks, 3 * len(ws) * _nbytes((tm, tn), F32))),
        name=name,
    )(x, *ws, *row_aux, *col_aux)
    return res


def _mm_res_kernel(*refs, alpha, has_next):
    if has_next:
        h_ref, w_ref, x_ref, gp_ref, gn_ref, xo_ref, no_ref = refs
    else:
        h_ref, w_ref, x_ref, gp_ref, xo_ref = refs
    y = jnp.dot(h_ref[...], w_ref[...], preferred_element_type=F32)
    xo = x_ref[...] + alpha * _rms(y, gp_ref[...])
    xo_ref[...] = xo
    if has_next:
        no_ref[...] = _rms(xo, gn_ref[...]).astype(no_ref.dtype)


def mm_res(h, w, x, g_post, g_next, *, alpha, tm, name):
    m, kdim = h.shape
    d = w.shape[1]
    assert m % tm == 0
    has_next = g_next is not None
    once = pl.Buffered(1)
    in_specs = [
        pl.BlockSpec((tm, kdim), lambda i: (i, 0)),
        pl.BlockSpec((kdim, d), lambda i: (0, 0), pipeline_mode=once),
        pl.BlockSpec((tm, d), lambda i: (i, 0)),
        pl.BlockSpec((1, d), lambda i: (0, 0)),
    ]
    args = [h, w, x, g_post.reshape(1, d)]
    out_specs = [pl.BlockSpec((tm, d), lambda i: (i, 0))]
    out_shape = [jax.ShapeDtypeStruct((m, d), F32)]
    streamed = [_nbytes((tm, kdim), BF16), 2 * _nbytes((tm, d), F32)]
    if has_next:
        in_specs.append(pl.BlockSpec((1, d), lambda i: (0, 0)))
        args.append(g_next.reshape(1, d))
        out_specs.append(pl.BlockSpec((tm, d), lambda i: (i, 0)))
        out_shape.append(jax.ShapeDtypeStruct((m, d), BF16))
        streamed.append(_nbytes((tm, d), BF16))
    res = pl.pallas_call(
        functools.partial(_mm_res_kernel, alpha=alpha, has_next=has_next),
        grid=(m // tm,),
        in_specs=in_specs,
        out_specs=out_specs,
        out_shape=out_shape,
        compiler_params=pltpu.CompilerParams(
            dimension_semantics=("arbitrary",),
            vmem_limit_bytes=_vmem_limit(streamed, _nbytes((kdim, d), BF16) + 3 * _nbytes((tm, d), F32))),
        name=name,
    )(*args)
    return (res[0], res[1]) if has_next else (res[0], None)


def _merge_kernel(a_ref, b_ref, c_ref, g_ref, wa_ref, wb_ref, wc_ref, o_ref, *, d):
    g = g_ref[...].astype(F32)
    acc = g[:, :d] * jnp.dot(a_ref[...], wa_ref[...], preferred_element_type=F32)
    acc += g[:, d:2 * d] * jnp.dot(b_ref[...], wb_ref[...], preferred_element_type=F32)
    acc += g[:, 2 * d:] * jnp.dot(c_ref[...], wc_ref[...], preferred_element_type=F32)
    o_ref[...] = acc.astype(o_ref.dtype)


def merge_branches(a, b, c, g, wa, wb, wc, *, tm):
    m, kw = a.shape
    d = wa.shape[1]
    row = lambda w: pl.BlockSpec((tm, w), lambda i: (i, 0))
    full = pl.BlockSpec((kw, d), lambda i: (0, 0))
    blocks = [3 * _nbytes((tm, kw), BF16), _nbytes((tm, 3 * d), BF16), 3 * _nbytes((kw, d), BF16), _nbytes((tm, d), BF16)]
    return pl.pallas_call(
        functools.partial(_merge_kernel, d=d),
        grid=(m // tm,),
        in_specs=[row(kw), row(kw), row(kw), row(3 * d), full, full, full],
        out_specs=row(d),
        out_shape=jax.ShapeDtypeStruct((m, d), BF16),
        compiler_params=pltpu.CompilerParams(
            dimension_semantics=("arbitrary",),
            vmem_limit_bytes=_vmem_limit(blocks, 4 * _nbytes((tm, d), F32))),
        name="merge_branches",
    )(a, b, c, g, wa, wb, wc)


def _flash_kernel(q_ref, k_ref, v_ref, o_ref, *, tq, scale):
    qi = pl.program_id(2)
    q = q_ref[...]
    c2 = scale * np.log2(np.e)

    def step(kb, carry, masked):
        m_prev, acc = carry
        start = pl.multiple_of(kb * tq, tq)
        ks = k_ref[pl.ds(start, tq), :]
        vs = v_ref[pl.ds(start, tq), :]
        s = lax.dot_general(q, ks, (((1,), (1,)), ((), ())), preferred_element_type=F32)
        if masked:
            r = lax.broadcasted_iota(jnp.int32, (tq, tq), 0)
            c = lax.broadcasted_iota(jnp.int32, (tq, tq), 1)
            s = jnp.where(c <= r, s, -jnp.inf)
        m_new = jnp.maximum(m_prev, jnp.max(s, axis=-1, keepdims=True))
        a = jnp.exp2((m_prev - m_new) * c2)
        p = jnp.exp2((s - m_new) * c2)
        acc = a * acc + jnp.dot(p.astype(BF16), vs, preferred_element_type=F32)
        return m_new, acc

    init = (jnp.full((tq, 1), -jnp.inf, F32), jnp.zeros((tq, v_ref.shape[1]), F32))
    carry = lax.fori_loop(0, qi, lambda kb, c: step(kb, c, False), init)
    _, acc = step(qi, carry, True)
    o_ref[...] = (acc[:, :V_DIM] / acc[:, V_DIM:V_DIM + 1]).astype(o_ref.dtype)


def mla_prompt_attention(q, kfull, v, *, batch, seq, tq):
    nq = seq // tq
    blocks = [_nbytes((tq, Q_HEAD_PAD), BF16), 2 * _nbytes((seq, Q_HEAD_PAD), BF16), _nbytes((tq, V_DIM), BF16)]
    return pl.pallas_call(
        functools.partial(_flash_kernel, tq=tq, scale=MLA_SCALE),
        grid=(batch, MLA_HEADS, nq),
        in_specs=[
            pl.BlockSpec((tq, Q_HEAD_PAD), lambda b, h, i: (b * nq + i, h)),
            pl.BlockSpec((seq, Q_HEAD_PAD), lambda b, h, i: (b, h)),
            pl.BlockSpec((seq, Q_HEAD_PAD), lambda b, h, i: (b, h)),
        ],
        out_specs=pl.BlockSpec((tq, V_DIM), lambda b, h, i: (b * nq + i, h)),
        out_shape=jax.ShapeDtypeStruct((batch * seq, MLA_WIDTH), BF16),
        compiler_params=pltpu.CompilerParams(
            dimension_semantics=("arbitrary", "arbitrary", "arbitrary"),
            vmem_limit_bytes=_vmem_limit(blocks, 6 * _nbytes((tq, tq), F32))),
        name="mla_prompt_attention",
    )(q, kfull, v)


def _dec_kernel(pt_ref, q_ref, new_ref, cache_ref, o_ref, buf, kb_ref, s_ref, p_ref, sem, *, layer, group, n_groups,
                t_new, scale):
    b = pl.program_id(0)
    rows_q = q_ref.shape[1]

    def page_copy(bb, g, slot, p):
        page = pt_ref[bb, g * group + p]
        return pltpu.make_async_copy(cache_ref.at[layer, page], buf.at[slot, p], sem.at[slot])

    def start(bb, g, slot):
        for p in range(group):
            page_copy(bb, g, slot, p).start()

    def wait(bb, g, slot):
        for p in range(group):
            page_copy(bb, g, slot, p).wait()

    @pl.when(b == 0)
    def _():
        start(0, 0, 0)

    q = q_ref[0].astype(BF16)
    m_run = jnp.full((rows_q, 1), -jnp.inf, F32)
    l_run = jnp.zeros((rows_q, 1), F32)
    acc = jnp.zeros((rows_q, KV_LORA), F32)
    for g in range(n_groups):
        slot = g % 2
        if g + 1 < n_groups:
            start(b, g + 1, 1 - slot)
        else:
            @pl.when(b + 1 < pl.num_programs(0))
            def _():
                start(b + 1, 0, 1 - slot)
        wait(b, g, slot)

        def scores(p, carry, slot=slot):
            kp = buf[slot, p].astype(BF16)
            kb_ref[p] = kp[:KV_LORA]
            s_ref[p] = jnp.dot(q, kp, preferred_element_type=F32) * scale
            return carry

        lax.fori_loop(0, group, scores, 0, unroll=4)
        s3 = s_ref[...]
        m_new = jnp.maximum(m_run, jnp.max(jnp.max(s3, axis=0), axis=-1, keepdims=True))
        a = jnp.exp(m_run - m_new)
        p3 = jnp.exp(s3 - m_new[None])
        l_run = a * l_run + jnp.sum(jnp.sum(p3, axis=0), axis=-1, keepdims=True)
        p_ref[...] = p3.astype(BF16)
        m_run = m_new

        def values(p, acc_in):
            return acc_in + lax.dot_general(p_ref[p], kb_ref[p], (((1,), (1,)), ((), ())),
                                            preferred_element_type=F32)

        acc = lax.fori_loop(0, group, values, a * acc, unroll=4)
    keys = new_ref[0].astype(BF16)
    s = lax.dot_general(q, keys, (((1,), (1,)), ((), ())), preferred_element_type=F32) * scale
    t_of_row = lax.broadcasted_iota(jnp.int32, (rows_q, t_new), 0) % t_new
    j = lax.broadcasted_iota(jnp.int32, (rows_q, t_new), 1)
    s = jnp.where(j <= t_of_row, s, -jnp.inf)
    m_new = jnp.maximum(m_run, jnp.max(s, axis=-1, keepdims=True))
    a = jnp.exp(m_run - m_new)
    pn = jnp.exp(s - m_new)
    l_fin = a * l_run + jnp.sum(pn, axis=-1, keepdims=True)
    acc = a * acc + jnp.dot(pn.astype(BF16), keys[:, :KV_LORA], preferred_element_type=F32)
    o_ref[0] = acc / l_fin


def mla_sample_attention(qcat, rows_new, cache_t, page_table, *, layer, group):
    nb, rows_q, _ = qcat.shape
    t_new = rows_new.shape[1]
    n_pages = page_table.shape[1]
    n_groups = n_pages // group
    assert n_pages % group == 0 and n_groups % 2 == 0
    grid_spec = pltpu.PrefetchScalarGridSpec(
        num_scalar_prefetch=1,
        grid=(nb,),
        in_specs=[
            pl.BlockSpec((1, rows_q, CACHE_DIM), lambda b, pt: (b, 0, 0)),
            pl.BlockSpec((1, t_new, CACHE_DIM), lambda b, pt: (b, 0, 0)),
            pl.BlockSpec(memory_space=pl.ANY),
        ],
        out_specs=pl.BlockSpec((1, rows_q, KV_LORA), lambda b, pt: (b, 0, 0)),
        scratch_shapes=[
            pltpu.VMEM((2, group, CACHE_DIM, PAGE_SIZE), F32),
            pltpu.VMEM((group, KV_LORA, PAGE_SIZE), BF16),
            pltpu.VMEM((group, rows_q, PAGE_SIZE), F32),
            pltpu.VMEM((group, rows_q, PAGE_SIZE), BF16),
            pltpu.SemaphoreType.DMA((2,)),
        ],
    )
    scratch_bytes = (_nbytes((2, group, CACHE_DIM, PAGE_SIZE), F32) + _nbytes((group, KV_LORA, PAGE_SIZE), BF16)
                     + 6 * _nbytes((group, rows_q, PAGE_SIZE), F32))
    return pl.pallas_call(
        functools.partial(_dec_kernel, layer=layer, group=group, n_groups=n_groups, t_new=t_new, scale=MLA_SCALE),
        grid_spec=grid_spec,
        out_shape=jax.ShapeDtypeStruct((nb, rows_q, KV_LORA), F32),
        compiler_params=pltpu.CompilerParams(
            dimension_semantics=("arbitrary",),
            vmem_limit_bytes=_vmem_limit([_nbytes((rows_q, CACHE_DIM), F32)], scratch_bytes)),
        name="mla_sample_attention",
    )(page_table, qcat, rows_new, cache_t)


def _qlat_kernel(q_ref, w_ref, o_ref, *, nb, t):
    q = q_ref[...]
    lat = jnp.dot(q[:, :QK_NOPE], w_ref[0], preferred_element_type=F32)
    cat = jnp.concatenate([lat, q[:, QK_NOPE:QK_NOPE + QK_ROPE].astype(F32)], axis=-1)
    o_ref[...] = cat.reshape(nb, 1, t, CACHE_DIM).astype(o_ref.dtype)


def sample_query_latent(q, w_lat, *, row0, nb, t):
    m_s = nb * t
    blk0 = row0 // m_s
    return pl.pallas_call(
        functools.partial(_qlat_kernel, nb=nb, t=t),
        grid=(MLA_HEADS,),
        in_specs=[
            pl.BlockSpec((m_s, Q_HEAD_PAD), lambda h: (blk0, h)),
            pl.BlockSpec((1, QK_NOPE, KV_LORA), lambda h: (h, 0, 0)),
        ],
        out_specs=pl.BlockSpec((nb, 1, t, CACHE_DIM), lambda h: (0, h, 0, 0)),
        out_shape=jax.ShapeDtypeStruct((nb, MLA_HEADS, t, CACHE_DIM), F32),
        compiler_params=pltpu.CompilerParams(dimension_semantics=("arbitrary",)),
        name="sample_query_latent",
    )(q, w_lat)


def _oproj_kernel(o_ref, w_ref, a_ref, *, nb, t):
    o = o_ref[...].reshape(nb * t, KV_LORA).astype(BF16)
    a_ref[...] = jnp.dot(o, w_ref[0], preferred_element_type=F32).astype(a_ref.dtype)


def sample_value_proj(o_lat, w_uv, *, nb, t):
    return pl.pallas_call(
        functools.partial(_oproj_kernel, nb=nb, t=t),
        grid=(MLA_HEADS,),
        in_specs=[
            pl.BlockSpec((nb, 1, t, KV_LORA), lambda h: (0, h, 0, 0)),
            pl.BlockSpec((1, KV_LORA, V_DIM), lambda h: (h, 0, 0)),
        ],
        out_specs=pl.BlockSpec((nb * t, V_DIM), lambda h: (0, h)),
        out_shape=jax.ShapeDtypeStruct((nb * t, MLA_WIDTH), BF16),
        compiler_params=pltpu.CompilerParams(dimension_semantics=("arbitrary",)),
        name="sample_value_proj",
    )(o_lat, w_uv)


def _hgrn_kernel(*refs, layer, sub, n_sub, nseq, has_init, wide):
    if has_init:
        lbl_ref, hq_ref, hf_ref, hi_ref, hg_ref, gain_ref, s0_ref, b_ref, sout_ref, st_ref = refs
    else:
        lbl_ref, hq_ref, hf_ref, hi_ref, hg_ref, gain_ref, b_ref, sout_ref, st_ref = refs
    rows = nseq * n_sub * sub
    logits = lbl_ref[...]
    e = jnp.exp(logits - jnp.max(logits, axis=0, keepdims=True))
    wsm = e / jnp.sum(e, axis=0, keepdims=True)
    lb = jnp.sum(wsm[:layer + 1], axis=0, keepdims=True) - wsm[0:1]

    f = lb + (1.0 - lb) * jax.nn.sigmoid(hf_ref[...])
    logf = jnp.log(f)
    kk = 1.0 - f
    q = jax.nn.silu(hq_ref[...])
    v = hi_ref[...]
    row = lax.broadcasted_iota(jnp.int32, (rows, HG_DK), 0)

    def block_cumsum(width):
        in_blk = jnp.bitwise_and(row, width - 1)
        acc = logf
        shift = 1
        while shift < width:
            acc = acc + jnp.where(in_blk >= shift, pltpu.roll(acc, shift, 0), 0.0)
            shift *= 2
        return acc

    def lower_tri(width):
        return (lax.broadcasted_iota(jnp.int32, (width, width), 1)
                <= lax.broadcasted_iota(jnp.int32, (width, width), 0))

    def exact_blocks(st, lo, hi):
        bcum = block_cumsum(sub)
        qt = q * jnp.exp(bcum)
        tri = lower_tri(sub)
        outs = []
        for r0 in range(lo, hi, sub):
            sl = slice(r0, r0 + sub)
            bj, qj, kj, vj = bcum[sl], q[sl], kk[sl], v[sl].astype(BF16)
            o_inter = lax.dot_general(qt[sl].astype(BF16), st.astype(BF16), (((1,), (1,)), ((), ())),
                                      preferred_element_type=F32)
            diff = jnp.minimum(bj[:, None, :] - bj[None, :, :], 0.0)
            att = jnp.sum(qj[:, None, :] * kj[None, :, :] * jnp.exp(diff), axis=-1)
            att = jnp.where(tri, att, 0.0)
            o_intra = jnp.dot(att.astype(BF16), vj, preferred_element_type=F32)
            bl = bj[sub - 1:sub, :]
            kt = (kj * jnp.exp(bl - bj)).astype(BF16)
            upd = lax.dot_general(vj, kt, (((0,), (0,)), ((), ())), preferred_element_type=F32)
            st = st * jnp.exp(bl) + upd
            outs.append(o_intra + o_inter)
        return jnp.concatenate(outs, axis=0), st

    def factored_blocks(st, bcum, lo, hi):
        qt = q * jnp.exp(bcum)
        kt = kk * jnp.exp(-bcum)
        tri = lower_tri(wide)
        outs = []
        for r0 in range(lo, hi, wide):
            sl = slice(r0, r0 + wide)
            qj, vj = qt[sl].astype(BF16), v[sl].astype(BF16)
            att = lax.dot_general(qj, kt[sl].astype(BF16), (((1,), (1,)), ((), ())), preferred_element_type=F32)
            att = jnp.where(tri, att, 0.0).astype(BF16)
            o_blk = jnp.dot(att, vj, preferred_element_type=F32)
            o_blk += lax.dot_general(qj, st.astype(BF16), (((1,), (1,)), ((), ())), preferred_element_type=F32)
            bl = bcum[r0 + wide - 1:r0 + wide, :]
            kl = (kk[sl] * jnp.exp(bl - bcum[sl])).astype(BF16)
            upd = lax.dot_general(vj, kl, (((0,), (0,)), ((), ())), preferred_element_type=F32)
            st = st * jnp.exp(bl) + upd
            outs.append(o_blk)
        return jnp.concatenate(outs, axis=0), st

    if not has_init:
        @pl.when(pl.program_id(2) == 0)
        def _():
            st_ref[...] = jnp.zeros_like(st_ref)

    per_seq = n_sub * sub
    outs = []
    for s in range(nseq):
        st0 = s0_ref[s, 0].T if has_init else st_ref[...]
        lo, hi = s * per_seq, (s + 1) * per_seq
        if wide is None:
            o_s, st = exact_blocks(st0, lo, hi)
        else:
            bwide = block_cumsum(wide)
            safe = jnp.min(bwide) >= -HGRN_SAFE_LOG
            o_s, st = lax.cond(safe, lambda s_in: factored_blocks(s_in, bwide, lo, hi),
                               lambda s_in: exact_blocks(s_in, lo, hi), st0)
        outs.append(o_s)
        if has_init:
            sout_ref[s, 0] = st.T
        else:
            st_ref[...] = st

            @pl.when(pl.program_id(2) == pl.num_programs(2) - 1)
            def _():
                sout_ref[0, 0] = st.T
    o = outs[0] if nseq == 1 else jnp.concatenate(outs, axis=0)
    b_ref[...] = (_rms(o, gain_ref[...]) * jax.nn.silu(hg_ref[...])).astype(b_ref.dtype)


def hgrn_mixer(hproj, lb_logits, gain, state0, *, layer, row0, nseq_total, seq_len, blk_len, nseq, sub, wide=None):
    has_init = state0 is not None
    n_sub = blk_len // sub
    nblk = seq_len // blk_len
    rows = nseq * blk_len
    assert row0 % rows == 0 and nseq_total % nseq == 0 and (nseq == 1 or nblk == 1)
    r0 = row0 // rows
    h_ = HG_HEADS
    col = lambda p: pl.BlockSpec((rows, HG_DK), lambda s, h, c: (r0 + s * nblk + c, p * h_ + h))
    in_specs = [
        pl.BlockSpec((DEPTH, HG_DK), lambda s, h, c: (0, h)),
        col(0), col(1), col(2), col(3),
        pl.BlockSpec((1, HG_DV), lambda s, h, c: (0, h)),
    ]
    args = [lb_logits, hproj, hproj, hproj, hproj, gain.reshape(1, HG_WIDTH)]
    if has_init:
        in_specs.append(pl.BlockSpec((nseq, 1, HG_DK, HG_DV), lambda s, h, c: (s, h, 0, 0)))
        args.append(state0)
    out_specs = [
        pl.BlockSpec((rows, HG_DV), lambda s, h, c: (s * nblk + c, h)),
        pl.BlockSpec((nseq, 1, HG_DK, HG_DV), lambda s, h, c: (s, h, 0, 0)),
    ]
    out_shape = [
        jax.ShapeDtypeStruct((nseq_total * seq_len, HG_WIDTH), BF16),
        jax.ShapeDtypeStruct((nseq_total, HG_HEADS, HG_DK, HG_DV), F32),
    ]
    return pl.pallas_call(
        functools.partial(_hgrn_kernel, layer=layer, sub=sub, n_sub=n_sub, nseq=nseq, has_init=has_init, wide=wide),
        grid=(nseq_total // nseq, HG_HEADS, nblk),
        in_specs=in_specs,
        out_specs=out_specs,
        out_shape=out_shape,
        scratch_shapes=[pltpu.VMEM((HG_DV, HG_DK), F32)],
        compiler_params=pltpu.CompilerParams(
            dimension_semantics=("arbitrary", "arbitrary", "arbitrary"),
            vmem_limit_bytes=32 << 20),
        name="hgrn_prompt" if not has_init else "hgrn_sample",
    )(*args)


def _gmlp_kernel(gu_ref, vn_ref, w_ref, bias_ref, o_ref, *, n_chunks):
    for c in range(n_chunks):
        rs = slice(c * GM_CHUNK, (c + 1) * GM_CHUNK)
        vn = vn_ref[rs, :].astype(BF16)
        parts = []
        for g in range(GM_GROUPS):
            cs = slice(g * GM_GROUP_DIM, (g + 1) * GM_GROUP_DIM)
            parts.append(jnp.dot(w_ref[0, g], vn[:, cs], preferred_element_type=F32))
        mixed = jnp.concatenate(parts, axis=-1) + bias_ref[0]
        o_ref[rs, :] = (gu_ref[rs, :].astype(F32) * mixed).astype(o_ref.dtype)


def gmlp_mix(gu_act, vn, wmix, bias, *, tm, prompt_rows):
    m = gu_act.shape[0]
    first_sample_tile = prompt_rows // tm
    kind = lambda i: jnp.where(i >= first_sample_tile, 1, 0)
    return pl.pallas_call(
        functools.partial(_gmlp_kernel, n_chunks=tm // GM_CHUNK),
        grid=(m // tm,),
        in_specs=[
            pl.BlockSpec((tm, GM_WIDTH), lambda i: (i, 0)),
            pl.BlockSpec((tm, GM_WIDTH), lambda i: (i, 0)),
            pl.BlockSpec((1, GM_GROUPS, GM_CHUNK, GM_CHUNK), lambda i: (kind(i), 0, 0, 0)),
            pl.BlockSpec((1, GM_CHUNK, GM_WIDTH), lambda i: (kind(i), 0, 0)),
        ],
        out_specs=pl.BlockSpec((tm, GM_WIDTH), lambda i: (i, 0)),
        out_shape=jax.ShapeDtypeStruct((m, GM_WIDTH), BF16),
        compiler_params=pltpu.CompilerParams(dimension_semantics=("arbitrary",), vmem_limit_bytes=32 << 20),
        name="gmlp_mix",
    )(gu_act, vn, wmix, bias)


def _rope_tile(t, cos_t, sin_t):
    return t * cos_t + pltpu.roll(t, QK_ROPE, 1) * sin_t


def _epi_ffn_up(accs, row_aux, col_aux):
    return [jax.nn.silu(accs[0]) * accs[1]]


def _epi_small(accs, row_aux, col_aux):
    y = accs[0]
    cos_t, sin_t = row_aux
    qg, kvg = col_aux
    cqn = _rms(y[:, :Q_LORA], qg)
    ckvn = _rms(y[:, Q_LORA:Q_LORA + KV_LORA], kvg)
    roped = _rope_tile(y[:, Q_LORA + KV_LORA:], cos_t, sin_t)
    return [cqn, jnp.concatenate([ckvn, roped[:, :QK_ROPE]], axis=-1)]


def _epi_identity(accs, row_aux, col_aux):
    return [accs[0]]


def _epi_gelu(accs, row_aux, col_aux):
    return [jax.nn.gelu(accs[0])]


def _epi_gelu_ln(accs, row_aux, col_aux):
    y = jax.nn.gelu(accs[0])
    g, b = col_aux
    mu = jnp.mean(y, axis=-1, keepdims=True)
    var = jnp.mean(jnp.square(y - mu), axis=-1, keepdims=True)
    return [(y - mu) * lax.rsqrt(var + EPS) * g + b]


def _epi_sigmoid(accs, row_aux, col_aux):
    return [jax.nn.sigmoid(accs[0])]


def _epi_q_rope(accs, row_aux, col_aux):
    y = accs[0]
    cos_t, sin_t = row_aux
    parts = []
    for h in range(MLA_HEADS):
        base = h * Q_HEAD_PAD
        parts.append(y[:, base:base + QK_NOPE])
        parts.append(_rope_tile(y[:, base + QK_NOPE:base + Q_HEAD_PAD], cos_t, sin_t))
    return [jnp.concatenate(parts, axis=-1)]


def _epi_kv_split(accs, row_aux, col_aux):
    n_k = MLA_HEADS * Q_HEAD_PAD
    vals = accs[0][:, n_k:]
    col = lax.broadcasted_iota(jnp.int32, vals.shape, 1)
    ones_col = jnp.bitwise_and(col, Q_HEAD_PAD - 1) == V_DIM
    return [accs[0][:, :n_k], jnp.where(ones_col, 1.0, vals)]


def _rot_cols(w):
    half = QK_ROPE // 2
    return jnp.concatenate([-w[..., half:], w[..., :half]], axis=-1)


def _prep_layer(l, w_in, w_uq, w_ukv, gmlp_w_s, gmlp_b_s, t_sample):
    wi = w_in[l]
    o_kr = Q_LORA + KV_LORA
    o_h = o_kr + QK_ROPE
    w_small = jnp.concatenate([wi[:, :o_h], _rot_cols(wi[:, o_kr:o_h])], axis=-1).astype(BF16)
    w_hgrn = wi[:, o_h:o_h + 4 * HG_WIDTH].astype(BF16)
    o_g = o_h + 4 * HG_WIDTH
    w_gu = wi[:, o_g:o_g + GM_WIDTH].astype(BF16)
    w_gv = wi[:, o_g + GM_WIDTH:o_g + 2 * GM_WIDTH].astype(BF16)
    w_gate = wi[:, o_g + 2 * GM_WIDTH:].astype(BF16)

    uq = w_uq[l].reshape(Q_LORA, MLA_HEADS, QK_NOPE + QK_ROPE)
    uq_rope = uq[..., QK_NOPE:]
    w_q = jnp.concatenate([uq[..., :QK_NOPE], uq_rope, _rot_cols(uq_rope)], axis=-1)
    w_q = w_q.reshape(Q_LORA, MLA_HEADS * Q_HEAD_PAD).astype(BF16)

    ukv = w_ukv[l].reshape(KV_LORA, MLA_HEADS, QK_NOPE + V_DIM)
    w_uk, w_uv = ukv[..., :QK_NOPE], ukv[..., QK_NOPE:]
    k_top = jnp.concatenate([w_uk, jnp.zeros((KV_LORA, MLA_HEADS, Q_HEAD_PAD - QK_NOPE), F32)], axis=-1)
    eye = jnp.concatenate([jnp.zeros((QK_ROPE, QK_NOPE), F32), jnp.eye(QK_ROPE, dtype=F32),
                           jnp.zeros((QK_ROPE, Q_HEAD_PAD - QK_NOPE - QK_ROPE), F32)], axis=-1)
    k_bot = jnp.broadcast_to(eye[:, None, :], (QK_ROPE, MLA_HEADS, Q_HEAD_PAD))
    w_k = jnp.concatenate([k_top, k_bot], axis=0).reshape(CACHE_DIM, MLA_HEADS * Q_HEAD_PAD).astype(BF16)
    w_v = jnp.concatenate([w_uv, jnp.zeros((KV_LORA, MLA_HEADS, Q_HEAD_PAD - V_DIM), F32)], axis=-1)
    w_v = jnp.concatenate([w_v, jnp.zeros((QK_ROPE, MLA_HEADS, Q_HEAD_PAD), F32)], axis=0)
    w_v = w_v.reshape(CACHE_DIM, MLA_HEADS * Q_HEAD_PAD).astype(BF16)
    w_lat = jnp.transpose(w_uk, (1, 2, 0)).astype(BF16)
    w_uvh = jnp.transpose(w_uv, (1, 0, 2)).astype(BF16)

    ws = gmlp_w_s[l]
    w_prompt = jnp.tril(ws)
    small = jnp.tril(ws[:, :t_sample, :t_sample])
    reps = GM_CHUNK // t_sample
    w_sample = jnp.einsum("ab,gts->gatbs", jnp.eye(reps, dtype=F32), small).reshape(GM_GROUPS, GM_CHUNK, GM_CHUNK)
    wmix = jnp.stack([w_prompt, w_sample]).astype(BF16)
    bs = gmlp_b_s[l]
    bias_p = jnp.repeat(bs.T[:, :, None], GM_GROUP_DIM, axis=2).reshape(GM_CHUNK, GM_WIDTH)
    bias_s = jnp.tile(jnp.repeat(bs[:, :t_sample].T[:, :, None], GM_GROUP_DIM, axis=2).reshape(t_sample, GM_WIDTH),
                      (reps, 1))
    bias = jnp.stack([bias_p, bias_s])
    return dict(w_small=w_small, w_hgrn=w_hgrn, w_gu=w_gu, w_gv=w_gv, w_gate=w_gate, w_q=w_q,
                w_kv=jnp.concatenate([w_k, w_v], axis=-1), w_lat=w_lat, w_uvh=w_uvh, wmix=wmix, bias=bias)


def _rope_tables(pos):
    half = QK_ROPE // 2
    inv = ROPE_THETA ** (-jnp.arange(half, dtype=F32) / half)
    ang = pos.astype(F32)[:, None] * inv[None, :]
    pad = jnp.zeros((pos.shape[0], LANES - QK_ROPE), F32)
    cos_t = jnp.concatenate([jnp.cos(ang), jnp.cos(ang), pad], axis=-1)
    sin_t = jnp.concatenate([jnp.sin(ang), jnp.sin(ang), pad], axis=-1)
    return cos_t, sin_t


def kernel(x_prompt, x_sample, cache_mla, state_hgrn, page_table, norm_gains, w_ffn1_gate, w_ffn1_up, w_ffn1_down, w_ffn2_gate, w_ffn2_up, w_ffn2_down, w_in, mla_q_norm, mla_kv_norm, w_uq, w_ukv, hgrn_lb_logits, hgrn_out_norm, gmlp_ln_g, gmlp_ln_b, gmlp_w_s, gmlp_b_s, w_br_mla, w_br_hgrn, w_br_gmlp, w_out):
    batch, seq, d = x_prompt.shape
    nb, t_s, _ = x_sample.shape
    m_p, m_s = batch * seq, nb * t_s
    m = m_p + m_s
    past = page_table.shape[1] * PAGE_SIZE
    tm = 1024
    while m_p % tm or m_s % tm:
        tm //= 2
    th = min(512, tm)
    tq4 = min(256, tm)
    assert tm >= GM_CHUNK and m_p % m_s == 0

    x = jnp.concatenate([x_prompt.reshape(m_p, d), x_sample.reshape(m_s, d)], axis=0)
    pos = jnp.concatenate([jnp.tile(jnp.arange(seq), batch), jnp.tile(past + jnp.arange(t_s), nb)])
    cos_t, sin_t = _rope_tables(pos)
    bf = lambda w: w.astype(BF16)
    cache_t = jnp.swapaxes(cache_mla, 2, 3)

    rows_all, s_p_all, s_s_all, vn_all = [], [], [], []
    xn = rmsnorm_cast(x, norm_gains[0, 0], tm=th)
    for l in range(DEPTH):
        ng = norm_gains[l]
        p = _prep_layer(l, w_in, w_uq, w_ukv, gmlp_w_s, gmlp_b_s, t_s)

        hmid = mm(xn, [bf(w_ffn1_gate[l]), bf(w_ffn1_up[l])], _epi_ffn_up, [(512, BF16)], tm=tm, tn=512,
                  name="ffn1_up")[0]
        x, hn = mm_res(hmid, bf(w_ffn1_down[l]), x, ng[1], ng[2], alpha=0.5, tm=tq4, name="ffn1_down")

        cqn, rows = mm(hn, [p["w_small"]], _epi_small, [(Q_LORA, BF16), (CACHE_DIM, F32)], tm=th,
                       tn=p["w_small"].shape[1], row_aux=(cos_t, sin_t),
                       col_aux=(mla_q_norm[l].reshape(1, -1), mla_kv_norm[l].reshape(1, -1)), name="in_proj_mla")
        hproj = mm(hn, [p["w_hgrn"]], _epi_identity, [(1024, F32)], tm=tm, tn=1024, name="in_proj_hgrn")[0]
        gu_act = mm(hn, [p["w_gu"]], _epi_gelu, [(GM_WIDTH, BF16)], tm=tm, tn=GM_WIDTH, name="in_proj_gu")[0]
        vn = mm(hn, [p["w_gv"]], _epi_gelu_ln, [(GM_WIDTH, F32)], tm=tm, tn=GM_WIDTH,
                col_aux=(gmlp_ln_g[l].reshape(1, -1), gmlp_ln_b[l].reshape(1, -1)), name="in_proj_gv")[0]
        gate = mm(hn, [p["w_gate"]], _epi_sigmoid, [(1024, BF16)], tm=tm, tn=1024, name="in_proj_gate")[0]

        q = mm(cqn, [p["w_q"]], _epi_q_rope, [(MLA_HEADS * Q_HEAD_PAD, BF16)], tm=th, tn=MLA_HEADS * Q_HEAD_PAD,
               row_aux=(cos_t, sin_t), name="mla_q")[0]
        kfull, vv = mm(rows, [p["w_kv"]], _epi_kv_split, [(MLA_HEADS * Q_HEAD_PAD, BF16), (MLA_HEADS * Q_HEAD_PAD, BF16)],
                       tm=th, tn=p["w_kv"].shape[1], rows=m_p, name="mla_kv")
        a_p = mla_prompt_attention(q, kfull, vv, batch=batch, seq=seq, tq=min(512, seq))

        qcat = sample_query_latent(q, p["w_lat"], row0=m_p, nb=nb, t=t_s)
        o_lat = mla_sample_attention(qcat.reshape(nb, MLA_HEADS * t_s, CACHE_DIM), rows[m_p:].reshape(nb, t_s, CACHE_DIM),
                                     cache_t, page_table, layer=l, group=32)
        a_s = sample_value_proj(o_lat.reshape(nb, MLA_HEADS, t_s, KV_LORA), p["w_uvh"], nb=nb, t=t_s)
        a = jnp.concatenate([a_p, a_s], axis=0)

        b_p, s_p = hgrn_mixer(hproj, hgrn_lb_logits, hgrn_out_norm[l], None, layer=l, row0=0, nseq_total=batch,
                              seq_len=seq, blk_len=256, nseq=1, sub=HGRN_SUB, wide=HGRN_WIDE)
        b_s, s_s = hgrn_mixer(hproj, hgrn_lb_logits, hgrn_out_norm[l], state_hgrn[l], layer=l, row0=m_p,
                              nseq_total=nb, seq_len=t_s, blk_len=t_s, nseq=8, sub=t_s)
        bmix = jnp.concatenate([b_p, b_s], axis=0)

        cmix = gmlp_mix(gu_act, vn, p["wmix"], p["bias"], tm=th, prompt_rows=m_p)

        merged = merge_branches(a, bmix, cmix, gate, bf(w_br_mla[l]), bf(w_br_hgrn[l]), bf(w_br_gmlp[l]), tm=th)
        x, hn4 = mm_res(merged, bf(w_out[l]), x, ng[3], ng[4], alpha=1.0, tm=th, name="merge_out")

        hmid = mm(hn4, [bf(w_ffn2_gate[l]), bf(w_ffn2_up[l])], _epi_ffn_up, [(512, BF16)], tm=tm, tn=512,
                  name="ffn2_up")[0]
        g_next = norm_gains[l + 1, 0] if l + 1 < DEPTH else None
        x, xn = mm_res(hmid, bf(w_ffn2_down[l]), x, ng[5], g_next, alpha=0.5, tm=tq4, name="ffn2_down")

        rows_all.append(rows)
        s_p_all.append(s_p)
        s_s_all.append(s_s)
        vn_all.append(vn[m_p:])

    rows_st = jnp.stack(rows_all)
    return (x[:m_p].reshape(batch, seq, d), x[m_p:].reshape(nb, t_s, d),
            rows_st[:, :m_p].reshape(DEPTH, batch, seq, CACHE_DIM), rows_st[:, m_p:].reshape(DEPTH, nb, t_s, CACHE_DIM),
            jnp.stack(s_p_all), jnp.stack(s_s_all), jnp.stack(vn_all).reshape(DEPTH, nb, t_s, GM_WIDTH))
```

```python
import functools

import jax
import jax.numpy as jnp
import numpy as np
from jax import lax
from jax.experimental import pallas as pl
from jax.experimental.pallas import tpu as pltpu

D_MODEL = 2048
DEPTH = 2
PAGE_SIZE = 128
MLA_HEADS = 8
QK_NOPE = 128
QK_ROPE = 64
V_DIM = 128
Q_LORA = 512
KV_LORA = 256
ROPE_THETA = 10000.0
MLA_SCALE = (QK_NOPE + QK_ROPE) ** -0.5
MLA_WIDTH = MLA_HEADS * V_DIM
CACHE_DIM = KV_LORA + QK_ROPE
HG_HEADS = 8
HG_DK = 128
HG_DV = 128
HG_WIDTH = HG_HEADS * HG_DV
GM_GROUPS = 8
GM_GROUP_DIM = 128
GM_CHUNK = 128
GM_WIDTH = GM_GROUPS * GM_GROUP_DIM
D_FF = 5632
N_BRANCH = 3
EPS = 1e-6

LANES = 128
Q_HEAD_PAD = 2 * LANES
VMEM_CAP_BYTES = 60000 * 1024
DEC_STREAMS = 4
HGRN_SUB = 16
HGRN_WIDE = 32
HGRN_SAFE_LOG = 60.0

F32 = jnp.float32
BF16 = jnp.bfloat16


def _vmem_limit(block_bytes, temp_bytes=0):
    est = 2 * sum(block_bytes) + temp_bytes + (4 << 20)
    return int(min(max(est, 16 << 20), VMEM_CAP_BYTES))


def _nbytes(shape, dtype):
    return int(np.prod(shape)) * jnp.dtype(dtype).itemsize


def _rms(y, g):
    return y * lax.rsqrt(jnp.mean(y * y, axis=-1, keepdims=True) + EPS) * g


def _rmsnorm_kernel(x_ref, g_ref, o_ref):
    o_ref[...] = _rms(x_ref[...], g_ref[...]).astype(o_ref.dtype)


def rmsnorm_cast(x, g, *, tm):
    m, d = x.shape
    return pl.pallas_call(
        _rmsnorm_kernel,
        grid=(m // tm,),
        in_specs=[pl.BlockSpec((tm, d), lambda i: (i, 0)), pl.BlockSpec((1, d), lambda i: (0, 0))],
        out_specs=pl.BlockSpec((tm, d), lambda i: (i, 0)),
        out_shape=jax.ShapeDtypeStruct((m, d), BF16),
        compiler_params=pltpu.CompilerParams(
            dimension_semantics=("arbitrary",),
            vmem_limit_bytes=_vmem_limit([_nbytes((tm, d), F32), _nbytes((tm, d), BF16)], _nbytes((tm, d), F32))),
        name="rmsnorm_cast",
    )(x, g.reshape(1, d))


def _mm_kernel(*refs, n_w, n_row, n_col, n_out, epi):
    x_ref = refs[0]
    w_refs = refs[1:1 + n_w]
    row_refs = refs[1 + n_w:1 + n_w + n_row]
    col_refs = refs[1 + n_w + n_row:1 + n_w + n_row + n_col]
    out_refs = refs[len(refs) - n_out:]
    x = x_ref[...].astype(BF16)
    accs = [jnp.dot(x, w[...], preferred_element_type=F32) for w in w_refs]
    outs = epi(accs, [r[...] for r in row_refs], [c[...] for c in col_refs])
    for o_ref, o in zip(out_refs, outs):
        o_ref[...] = o.astype(o_ref.dtype)


def mm(x, ws, epi, outs, *, tm, tn, rows=None, row_aux=(), col_aux=(), name):
    m = rows if rows is not None else x.shape[0]
    k = x.shape[1]
    n = ws[0].shape[1]
    nt = n // tn
    assert m % tm == 0 and n % tn == 0
    in_specs = [pl.BlockSpec((tm, k), lambda j, i: (i, 0))]
    in_specs += [pl.BlockSpec((k, tn), lambda j, i: (0, j)) for _ in ws]
    in_specs += [pl.BlockSpec((tm, a.shape[1]), lambda j, i: (i, 0)) for a in row_aux]
    in_specs += [pl.BlockSpec((1, a.shape[1] // nt), lambda j, i: (0, j)) for a in col_aux]
    out_specs = [pl.BlockSpec((tm, w), lambda j, i: (i, j)) for w, _ in outs]
    out_shape = [jax.ShapeDtypeStruct((m, w * nt), dt) for w, dt in outs]
    blocks = [_nbytes((tm, k), x.dtype)] + [_nbytes((k, tn), BF16)] * len(ws)
    blocks += [_nbytes((tm, a.shape[1]), a.dtype) for a in row_aux]
    blocks += [_nbytes((tm, w), dt) for w, dt in outs]
    res = pl.pallas_call(
        functools.partial(_mm_kernel, n_w=len(ws), n_row=len(row_aux), n_col=len(col_aux), n_out=len(outs), epi=epi),
        grid=(nt, m // tm),
        in_specs=in_specs,
        out_specs=out_specs,
        out_shape=out_shape,
        compiler_params=pltpu.CompilerParams(
            dimension_semantics=("arbitrary", "arbitrary"),
            vmem_limit_bytes=_vmem_limit(blocks, 3 * len(ws) * _nbytes((tm, tn), F32))),
        name=name,
    )(x, *ws, *row_aux, *col_aux)
    return res


def _mm_res_kernel(*refs, alpha, has_next):
    if has_next:
        h_ref, w_ref, x_ref, gp_ref, gn_ref, xo_ref, no_ref = refs
    else:
        h_ref, w_ref, x_ref, gp_ref, xo_ref = refs
    y = jnp.dot(h_ref[...], w_ref[...], preferred_element_type=F32)
    xo = x_ref[...] + alpha * _rms(y, gp_ref[...])
    xo_ref[...] = xo
    if has_next:
        no_ref[...] = _rms(xo, gn_ref[...]).astype(no_ref.dtype)


def mm_res(h, w, x, g_post, g_next, *, alpha, tm, name):
    m, kdim = h.shape
    d = w.shape[1]
    assert m % tm == 0
    has_next = g_next is not None
    once = pl.Buffered(1)
    in_specs = [
        pl.BlockSpec((tm, kdim), lambda i: (i, 0)),
        pl.BlockSpec((kdim, d), lambda i: (0, 0), pipeline_mode=once),
        pl.BlockSpec((tm, d), lambda i: (i, 0)),
        pl.BlockSpec((1, d), lambda i: (0, 0)),
    ]
    args = [h, w, x, g_post.reshape(1, d)]
    out_specs = [pl.BlockSpec((tm, d), lambda i: (i, 0))]
    out_shape = [jax.ShapeDtypeStruct((m, d), F32)]
    streamed = [_nbytes((tm, kdim), BF16), 2 * _nbytes((tm, d), F32)]
    if has_next:
        in_specs.append(pl.BlockSpec((1, d), lambda i: (0, 0)))
        args.append(g_next.reshape(1, d))
        out_specs.append(pl.BlockSpec((tm, d), lambda i: (i, 0)))
        out_shape.append(jax.ShapeDtypeStruct((m, d), BF16))
        streamed.append(_nbytes((tm, d), BF16))
    res = pl.pallas_call(
        functools.partial(_mm_res_kernel, alpha=alpha, has_next=has_next),
        grid=(m // tm,),
        in_specs=in_specs,
        out_specs=out_specs,
        out_shape=out_shape,
        compiler_params=pltpu.CompilerParams(
            dimension_semantics=("arbitrary",),
            vmem_limit_bytes=_vmem_limit(streamed, _nbytes((kdim, d), BF16) + 3 * _nbytes((tm, d), F32))),
        name=name,
    )(*args)
    return (res[0], res[1]) if has_next else (res[0], None)


def _merge_kernel(a_ref, b_ref, c_ref, g_ref, wa_ref, wb_ref, wc_ref, o_ref, *, d):
    g = g_ref[...].astype(F32)
    acc = g[:, :d] * jnp.dot(a_ref[...], wa_ref[...], preferred_element_type=F32)
    acc += g[:, d:2 * d] * jnp.dot(b_ref[...], wb_ref[...], preferred_element_type=F32)
    acc += g[:, 2 * d:] * jnp.dot(c_ref[...], wc_ref[...], preferred_element_type=F32)
    o_ref[...] = acc.astype(o_ref.dtype)


def merge_branches(a, b, c, g, wa, wb, wc, *, tm):
    m, kw = a.shape
    d = wa.shape[1]
    row = lambda w: pl.BlockSpec((tm, w), lambda i: (i, 0))
    full = pl.BlockSpec((kw, d), lambda i: (0, 0))
    blocks = [3 * _nbytes((tm, kw), BF16), _nbytes((tm, 3 * d), BF16), 3 * _nbytes((kw, d), BF16), _nbytes((tm, d), BF16)]
    return pl.pallas_call(
        functools.partial(_merge_kernel, d=d),
        grid=(m // tm,),
        in_specs=[row(kw), row(kw), row(kw), row(3 * d), full, full, full],
        out_specs=row(d),
        out_shape=jax.ShapeDtypeStruct((m, d), BF16),
        compiler_params=pltpu.CompilerParams(
            dimension_semantics=("arbitrary",),
            vmem_limit_bytes=_vmem_limit(blocks, 4 * _nbytes((tm, d), F32))),
        name="merge_branches",
    )(a, b, c, g, wa, wb, wc)


def _flash_kernel(q_ref, k_ref, v_ref, o_ref, *, tq, scale, hpb):
    qi = pl.program_id(2)
    c2 = scale * np.log2(np.e)
    qs = [q_ref[:, h * Q_HEAD_PAD:(h + 1) * Q_HEAD_PAD] for h in range(hpb)]

    def step(kb, carry, masked):
        start = pl.multiple_of(kb * tq, tq)
        new = []
        for h in range(hpb):
            m_prev, l_prev, acc = carry[h]
            ks = k_ref[pl.ds(start, tq), h * Q_HEAD_PAD:(h + 1) * Q_HEAD_PAD]
            vs = v_ref[pl.ds(start, tq), h * V_DIM:(h + 1) * V_DIM]
            s = lax.dot_general(qs[h], ks, (((1,), (1,)), ((), ())), preferred_element_type=F32)
            if masked:
                r = lax.broadcasted_iota(jnp.int32, (tq, tq), 0)
                c = lax.broadcasted_iota(jnp.int32, (tq, tq), 1)
                s = jnp.where(c <= r, s, -jnp.inf)
            m_new = jnp.maximum(m_prev, jnp.max(s, axis=-1, keepdims=True))
            a = jnp.exp2((m_prev - m_new) * c2)
            p = jnp.exp2((s - m_new) * c2)
            l_new = a * l_prev + jnp.sum(p, axis=-1, keepdims=True)
            acc = a * acc + jnp.dot(p.astype(BF16), vs, preferred_element_type=F32)
            new.append((m_new, l_new, acc))
        return tuple(new)

    one = (jnp.full((tq, 1), -jnp.inf, F32), jnp.zeros((tq, 1), F32), jnp.zeros((tq, V_DIM), F32))
    carry = lax.fori_loop(0, qi, lambda kb, c: step(kb, c, False), (one,) * hpb)
    final = step(qi, carry, True)
    o_ref[...] = jnp.concatenate([acc / l_fin for _, l_fin, acc in final], axis=-1).astype(o_ref.dtype)


def mla_prompt_attention(q, kfull, v, *, batch, seq, tq, hpb):
    nq = seq // tq
    assert MLA_HEADS % hpb == 0
    blocks = [_nbytes((tq, hpb * Q_HEAD_PAD), BF16), _nbytes((seq, hpb * Q_HEAD_PAD), BF16),
              _nbytes((seq, hpb * V_DIM), BF16), _nbytes((tq, hpb * V_DIM), BF16)]
    return pl.pallas_call(
        functools.partial(_flash_kernel, tq=tq, scale=MLA_SCALE, hpb=hpb),
        grid=(batch, MLA_HEADS // hpb, nq),
        in_specs=[
            pl.BlockSpec((tq, hpb * Q_HEAD_PAD), lambda b, h, i: (b * nq + i, h)),
            pl.BlockSpec((seq, hpb * Q_HEAD_PAD), lambda b, h, i: (b, h)),
            pl.BlockSpec((seq, hpb * V_DIM), lambda b, h, i: (b, h)),
        ],
        out_specs=pl.BlockSpec((tq, hpb * V_DIM), lambda b, h, i: (b * nq + i, h)),
        out_shape=jax.ShapeDtypeStruct((batch * seq, MLA_WIDTH), BF16),
        compiler_params=pltpu.CompilerParams(
            dimension_semantics=("arbitrary", "arbitrary", "arbitrary"),
            vmem_limit_bytes=_vmem_limit(blocks, 6 * hpb * _nbytes((tq, tq), F32))),
        name="mla_prompt_attention",
    )(q, kfull, v)


def _dec_kernel(pt_ref, q_ref, new_ref, cache_ref, o_ref, buf, sem, *, layer, group, n_groups, t_new, scale):
    b = pl.program_id(0)
    rows_q = q_ref.shape[1]

    def page_copy(bb, g, slot, p):
        page = pt_ref[bb, g * group + p]
        dst = buf.at[slot, :, pl.ds(p * PAGE_SIZE, PAGE_SIZE)]
        return pltpu.make_async_copy(cache_ref.at[layer, page], dst, sem.at[slot])

    def start(bb, g, slot):
        for p in range(group):
            page_copy(bb, g, slot, p).start()

    def wait(bb, g, slot):
        for p in range(group):
            page_copy(bb, g, slot, p).wait()

    @pl.when(b == 0)
    def _():
        start(0, 0, 0)

    def fold(state, s, vals, vals_dim):
        m_run, l_run, acc = state
        m_new = jnp.maximum(m_run, jnp.max(s, axis=-1, keepdims=True))
        a = jnp.exp(m_run - m_new)
        p = jnp.exp(s - m_new)
        l_new = a * l_run + jnp.sum(p, axis=-1, keepdims=True)
        acc = a * acc + lax.dot_general(p.astype(BF16), vals, (((1,), (vals_dim,)), ((), ())),
                                        preferred_element_type=F32)
        return m_new, l_new, acc

    q = q_ref[0].astype(BF16)
    empty = (jnp.full((rows_q, 1), -jnp.inf, F32), jnp.zeros((rows_q, 1), F32), jnp.zeros((rows_q, KV_LORA), F32))
    states = [empty] * DEC_STREAMS
    span = group * PAGE_SIZE // DEC_STREAMS
    for g in range(n_groups):
        slot = g % 2
        if g + 1 < n_groups:
            start(b, g + 1, 1 - slot)
        else:
            @pl.when(b + 1 < pl.num_programs(0))
            def _():
                start(b + 1, 0, 1 - slot)
        wait(b, g, slot)
        for c in range(DEC_STREAMS):
            keys_t = buf[slot, :, c * span:(c + 1) * span].astype(BF16)
            s = jnp.dot(q, keys_t, preferred_element_type=F32) * scale
            states[c] = fold(states[c], s, keys_t[:KV_LORA], 1)
    keys = new_ref[0].astype(BF16)
    s = lax.dot_general(q, keys, (((1,), (1,)), ((), ())), preferred_element_type=F32) * scale
    t_of_row = lax.broadcasted_iota(jnp.int32, (rows_q, t_new), 0) % t_new
    j = lax.broadcasted_iota(jnp.int32, (rows_q, t_new), 1)
    s = jnp.where(j <= t_of_row, s, -jnp.inf)
    states[0] = fold(states[0], s, keys[:, :KV_LORA], 0)
    m_all = functools.reduce(jnp.maximum, [st[0] for st in states])
    l_fin = sum(st[1] * jnp.exp(st[0] - m_all) for st in states)
    acc = sum(st[2] * jnp.exp(st[0] - m_all) for st in states)
    o_ref[0] = acc / l_fin


def mla_sample_attention(qcat, rows_new, cache_t, page_table, *, layer, group):
    nb, rows_q, _ = qcat.shape
    t_new = rows_new.shape[1]
    n_pages = page_table.shape[1]
    n_groups = n_pages // group
    assert n_pages % group == 0 and n_groups % 2 == 0
    grid_spec = pltpu.PrefetchScalarGridSpec(
        num_scalar_prefetch=1,
        grid=(nb,),
        in_specs=[
            pl.BlockSpec((1, rows_q, CACHE_DIM), lambda b, pt: (b, 0, 0)),
            pl.BlockSpec((1, t_new, CACHE_DIM), lambda b, pt: (b, 0, 0)),
            pl.BlockSpec(memory_space=pl.ANY),
        ],
        out_specs=pl.BlockSpec((1, rows_q, KV_LORA), lambda b, pt: (b, 0, 0)),
        scratch_shapes=[
            pltpu.VMEM((2, CACHE_DIM, group * PAGE_SIZE), F32),
            pltpu.SemaphoreType.DMA((2,)),
        ],
    )
    scratch_bytes = (_nbytes((2, CACHE_DIM, group * PAGE_SIZE), F32) + _nbytes((CACHE_DIM, group * PAGE_SIZE), F32)
                     + 6 * _nbytes((rows_q, group * PAGE_SIZE), F32))
    return pl.pallas_call(
        functools.partial(_dec_kernel, layer=layer, group=group, n_groups=n_groups, t_new=t_new, scale=MLA_SCALE),
        grid_spec=grid_spec,
        out_shape=jax.ShapeDtypeStruct((nb, rows_q, KV_LORA), F32),
        compiler_params=pltpu.CompilerParams(
            dimension_semantics=("arbitrary",),
            vmem_limit_bytes=_vmem_limit([_nbytes((rows_q, CACHE_DIM), F32)], scratch_bytes)),
        name="mla_sample_attention",
    )(page_table, qcat, rows_new, cache_t)


def _qlat_kernel(q_ref, w_ref, o_ref, *, nb, t):
    q = q_ref[...]
    lat = jnp.dot(q[:, :QK_NOPE], w_ref[0], preferred_element_type=F32)
    cat = jnp.concatenate([lat, q[:, QK_NOPE:QK_NOPE + QK_ROPE].astype(F32)], axis=-1)
    o_ref[...] = cat.reshape(nb, 1, t, CACHE_DIM).astype(o_ref.dtype)


def sample_query_latent(q, w_lat, *, row0, nb, t):
    m_s = nb * t
    blk0 = row0 // m_s
    return pl.pallas_call(
        functools.partial(_qlat_kernel, nb=nb, t=t),
        grid=(MLA_HEADS,),
        in_specs=[
            pl.BlockSpec((m_s, Q_HEAD_PAD), lambda h: (blk0, h)),
            pl.BlockSpec((1, QK_NOPE, KV_LORA), lambda h: (h, 0, 0)),
        ],
        out_specs=pl.BlockSpec((nb, 1, t, CACHE_DIM), lambda h: (0, h, 0, 0)),
        out_shape=jax.ShapeDtypeStruct((nb, MLA_HEADS, t, CACHE_DIM), F32),
        compiler_params=pltpu.CompilerParams(dimension_semantics=("arbitrary",)),
        name="sample_query_latent",
    )(q, w_lat)


def _oproj_kernel(o_ref, w_ref, a_ref, *, nb, t):
    o = o_ref[...].reshape(nb * t, KV_LORA).astype(BF16)
    a_ref[...] = jnp.dot(o, w_ref[0], preferred_element_type=F32).astype(a_ref.dtype)


def sample_value_proj(o_lat, w_uv, *, nb, t):
    return pl.pallas_call(
        functools.partial(_oproj_kernel, nb=nb, t=t),
        grid=(MLA_HEADS,),
        in_specs=[
            pl.BlockSpec((nb, 1, t, KV_LORA), lambda h: (0, h, 0, 0)),
            pl.BlockSpec((1, KV_LORA, V_DIM), lambda h: (h, 0, 0)),
        ],
        out_specs=pl.BlockSpec((nb * t, V_DIM), lambda h: (0, h)),
        out_shape=jax.ShapeDtypeStruct((nb * t, MLA_WIDTH), BF16),
        compiler_params=pltpu.CompilerParams(dimension_semantics=("arbitrary",)),
        name="sample_value_proj",
    )(o_lat, w_uv)


def _hgrn_kernel(*refs, layer, sub, n_sub, nseq, has_init, wide, hpb):
    if has_init:
        lbl_ref, hq_ref, hf_ref, hi_ref, hg_ref, gain_ref, s0_ref, b_ref, sout_ref, st_ref = refs
    else:
        lbl_ref, hq_ref, hf_ref, hi_ref, hg_ref, gain_ref, b_ref, sout_ref, st_ref = refs
    rows = nseq * n_sub * sub
    logits = lbl_ref[...]
    e = jnp.exp(logits - jnp.max(logits, axis=0, keepdims=True))
    wsm = e / jnp.sum(e, axis=0, keepdims=True)
    lb = jnp.sum(wsm[:layer + 1], axis=0, keepdims=True) - wsm[0:1]

    f = lb + (1.0 - lb) * jax.nn.sigmoid(hf_ref[...])
    logf = jnp.log(f)
    kk = 1.0 - f
    q = jax.nn.silu(hq_ref[...])
    v = hi_ref[...]
    row = lax.broadcasted_iota(jnp.int32, (rows, hpb * HG_DK), 0)

    def block_cumsum(width):
        in_blk = jnp.bitwise_and(row, width - 1)
        acc = logf
        shift = 1
        while shift < width:
            acc = acc + jnp.where(in_blk >= shift, pltpu.roll(acc, shift, 0), 0.0)
            shift *= 2
        return acc

    def lower_tri(width):
        return (lax.broadcasted_iota(jnp.int32, (width, width), 1)
                <= lax.broadcasted_iota(jnp.int32, (width, width), 0))

    def head(x, h):
        return x[:, h * HG_DK:(h + 1) * HG_DK]

    def exact_blocks(sts, lo, hi):
        bcum = block_cumsum(sub)
        qt = q * jnp.exp(bcum)
        tri = lower_tri(sub)
        sts = list(sts)
        outs = [[] for _ in range(hpb)]
        for r0 in range(lo, hi, sub):
            sl = slice(r0, r0 + sub)
            for h in range(hpb):
                bj, qj, kj = head(bcum[sl], h), head(q[sl], h), head(kk[sl], h)
                vj = head(v[sl], h).astype(BF16)
                o_inter = lax.dot_general(head(qt[sl], h).astype(BF16), sts[h].astype(BF16),
                                          (((1,), (1,)), ((), ())), preferred_element_type=F32)
                diff = jnp.minimum(bj[:, None, :] - bj[None, :, :], 0.0)
                att = jnp.sum(qj[:, None, :] * kj[None, :, :] * jnp.exp(diff), axis=-1)
                att = jnp.where(tri, att, 0.0)
                o_intra = jnp.dot(att.astype(BF16), vj, preferred_element_type=F32)
                bl = bj[sub - 1:sub, :]
                kt = (kj * jnp.exp(bl - bj)).astype(BF16)
                upd = lax.dot_general(vj, kt, (((0,), (0,)), ((), ())), preferred_element_type=F32)
                sts[h] = sts[h] * jnp.exp(bl) + upd
                outs[h].append(o_intra + o_inter)
        return tuple(jnp.concatenate(o_h, axis=0) for o_h in outs), tuple(sts)

    def factored_blocks(sts, bcum, lo, hi):
        qt = q * jnp.exp(bcum)
        kt = kk * jnp.exp(-bcum)
        tri = lower_tri(wide)
        sts = list(sts)
        outs = [[] for _ in range(hpb)]
        for r0 in range(lo, hi, wide):
            sl = slice(r0, r0 + wide)
            bl_all = bcum[r0 + wide - 1:r0 + wide, :]
            kl_all = kk[sl] * jnp.exp(bl_all - bcum[sl])
            for h in range(hpb):
                qj, vj = head(qt[sl], h).astype(BF16), head(v[sl], h).astype(BF16)
                att = lax.dot_general(qj, head(kt[sl], h).astype(BF16), (((1,), (1,)), ((), ())),
                                      preferred_element_type=F32)
                att = jnp.where(tri, att, 0.0).astype(BF16)
                o_blk = jnp.dot(att, vj, preferred_element_type=F32)
                o_blk += lax.dot_general(qj, sts[h].astype(BF16), (((1,), (1,)), ((), ())),
                                         preferred_element_type=F32)
                upd = lax.dot_general(vj, head(kl_all, h).astype(BF16), (((0,), (0,)), ((), ())),
                                      preferred_element_type=F32)
                sts[h] = sts[h] * jnp.exp(head(bl_all, h)) + upd
                outs[h].append(o_blk)
        return tuple(jnp.concatenate(o_h, axis=0) for o_h in outs), tuple(sts)

    if not has_init:
        @pl.when(pl.program_id(2) == 0)
        def _():
            st_ref[...] = jnp.zeros_like(st_ref)

    per_seq = n_sub * sub
    outs = []
    for s in range(nseq):
        st0 = tuple(s0_ref[s, h].T if has_init else st_ref[h] for h in range(hpb))
        lo, hi = s * per_seq, (s + 1) * per_seq
        if wide is None:
            o_s, sts = exact_blocks(st0, lo, hi)
        else:
            bwide = block_cumsum(wide)
            safe = jnp.min(bwide) >= -HGRN_SAFE_LOG
            o_s, sts = lax.cond(safe, lambda s_in: factored_blocks(s_in, bwide, lo, hi),
                                lambda s_in: exact_blocks(s_in, lo, hi), st0)
        outs.append(o_s)
        for h in range(hpb):
            if has_init:
                sout_ref[s, h] = sts[h].T
            else:
                st_ref[h] = sts[h]
        if not has_init:
            @pl.when(pl.program_id(2) == pl.num_programs(2) - 1)
            def _():
                for h in range(hpb):
                    sout_ref[0, h] = sts[h].T
    gain = gain_ref[...]
    normed = []
    for h in range(hpb):
        o_h = outs[0][h] if nseq == 1 else jnp.concatenate([o_s[h] for o_s in outs], axis=0)
        normed.append(_rms(o_h, head(gain, h)))
    o = normed[0] if hpb == 1 else jnp.concatenate(normed, axis=-1)
    b_ref[...] = (o * jax.nn.silu(hg_ref[...])).astype(b_ref.dtype)


def hgrn_mixer(hproj, lb_logits, gain, state0, *, layer, row0, nseq_total, seq_len, blk_len, nseq, sub, wide=None,
               hpb=1):
    has_init = state0 is not None
    n_sub = blk_len // sub
    nblk = seq_len // blk_len
    rows = nseq * blk_len
    assert row0 % rows == 0 and nseq_total % nseq == 0 and (nseq == 1 or nblk == 1)
    r0 = row0 // rows
    assert HG_HEADS % hpb == 0
    hg = HG_HEADS // hpb
    wcol = hpb * HG_DK
    col = lambda p: pl.BlockSpec((rows, wcol), lambda s, h, c: (r0 + s * nblk + c, p * hg + h))
    in_specs = [
        pl.BlockSpec((DEPTH, wcol), lambda s, h, c: (0, h)),
        col(0), col(1), col(2), col(3),
        pl.BlockSpec((1, wcol), lambda s, h, c: (0, h)),
    ]
    args = [lb_logits, hproj, hproj, hproj, hproj, gain.reshape(1, HG_WIDTH)]
    if has_init:
        in_specs.append(pl.BlockSpec((nseq, hpb, HG_DK, HG_DV), lambda s, h, c: (s, h, 0, 0)))
        args.append(state0)
    out_specs = [
        pl.BlockSpec((rows, wcol), lambda s, h, c: (s * nblk + c, h)),
        pl.BlockSpec((nseq, hpb, HG_DK, HG_DV), lambda s, h, c: (s, h, 0, 0)),
    ]
    out_shape = [
        jax.ShapeDtypeStruct((nseq_total * seq_len, HG_WIDTH), BF16),
        jax.ShapeDtypeStruct((nseq_total, HG_HEADS, HG_DK, HG_DV), F32),
    ]
    return pl.pallas_call(
        functools.partial(_hgrn_kernel, layer=layer, sub=sub, n_sub=n_sub, nseq=nseq, has_init=has_init, wide=wide,
                          hpb=hpb),
        grid=(nseq_total // nseq, hg, nblk),
        in_specs=in_specs,
        out_specs=out_specs,
        out_shape=out_shape,
        scratch_shapes=[pltpu.VMEM((hpb, HG_DV, HG_DK), F32)],
        compiler_params=pltpu.CompilerParams(
            dimension_semantics=("arbitrary", "arbitrary", "arbitrary"),
            vmem_limit_bytes=32 << 20),
        name="hgrn_prompt" if not has_init else "hgrn_sample",
    )(*args)


def _gmlp_kernel(gu_ref, vn_ref, w_ref, bias_ref, o_ref, *, n_chunks):
    for c in range(n_chunks):
        rs = slice(c * GM_CHUNK, (c + 1) * GM_CHUNK)
        vn = vn_ref[rs, :].astype(BF16)
        parts = []
        for g in range(GM_GROUPS):
            cs = slice(g * GM_GROUP_DIM, (g + 1) * GM_GROUP_DIM)
            parts.append(jnp.dot(w_ref[0, g], vn[:, cs], preferred_element_type=F32))
        mixed = jnp.concatenate(parts, axis=-1) + bias_ref[0]
        o_ref[rs, :] = (gu_ref[rs, :].astype(F32) * mixed).astype(o_ref.dtype)


def gmlp_mix(gu_act, vn, wmix, bias, *, tm, prompt_rows):
    m = gu_act.shape[0]
    first_sample_tile = prompt_rows // tm
    kind = lambda i: jnp.where(i >= first_sample_tile, 1, 0)
    return pl.pallas_call(
        functools.partial(_gmlp_kernel, n_chunks=tm // GM_CHUNK),
        grid=(m // tm,),
        in_specs=[
            pl.BlockSpec((tm, GM_WIDTH), lambda i: (i, 0)),
            pl.BlockSpec((tm, GM_WIDTH), lambda i: (i, 0)),
            pl.BlockSpec((1, GM_GROUPS, GM_CHUNK, GM_CHUNK), lambda i: (kind(i), 0, 0, 0)),
            pl.BlockSpec((1, GM_CHUNK, GM_WIDTH), lambda i: (kind(i), 0, 0)),
        ],
        out_specs=pl.BlockSpec((tm, GM_WIDTH), lambda i: (i, 0)),
        out_shape=jax.ShapeDtypeStruct((m, GM_WIDTH), BF16),
        compiler_params=pltpu.CompilerParams(dimension_semantics=("arbitrary",), vmem_limit_bytes=32 << 20),
        name="gmlp_mix",
    )(gu_act, vn, wmix, bias)


def _rope_tile(t, cos_t, sin_t):
    return t * cos_t + pltpu.roll(t, QK_ROPE, 1) * sin_t


def _epi_ffn_up(accs, row_aux, col_aux):
    return [jax.nn.silu(accs[0]) * accs[1]]


def _epi_small(accs, row_aux, col_aux):
    y = accs[0]
    cos_t, sin_t = row_aux
    qg, kvg = col_aux
    cqn = _rms(y[:, :Q_LORA], qg)
    ckvn = _rms(y[:, Q_LORA:Q_LORA + KV_LORA], kvg)
    roped = _rope_tile(y[:, Q_LORA + KV_LORA:], cos_t, sin_t)
    return [cqn, jnp.concatenate([ckvn, roped[:, :QK_ROPE]], axis=-1)]


def _epi_identity(accs, row_aux, col_aux):
    return [accs[0]]


def _epi_gelu(accs, row_aux, col_aux):
    return [jax.nn.gelu(accs[0])]


def _epi_gelu_ln(accs, row_aux, col_aux):
    y = jax.nn.gelu(accs[0])
    g, b = col_aux
    mu = jnp.mean(y, axis=-1, keepdims=True)
    var = jnp.mean(jnp.square(y - mu), axis=-1, keepdims=True)
    return [(y - mu) * lax.rsqrt(var + EPS) * g + b]


def _epi_sigmoid(accs, row_aux, col_aux):
    return [jax.nn.sigmoid(accs[0])]


def _epi_q_rope(accs, row_aux, col_aux):
    y = accs[0]
    cos_t, sin_t = row_aux
    parts = []
    for h in range(MLA_HEADS):
        base = h * Q_HEAD_PAD
        parts.append(y[:, base:base + QK_NOPE])
        parts.append(_rope_tile(y[:, base + QK_NOPE:base + Q_HEAD_PAD], cos_t, sin_t))
    return [jnp.concatenate(parts, axis=-1)]


def _epi_kv_split(accs, row_aux, col_aux):
    n_k = MLA_HEADS * Q_HEAD_PAD
    return [accs[0][:, :n_k], accs[0][:, n_k:]]


def _rot_cols(w):
    half = QK_ROPE // 2
    return jnp.concatenate([-w[..., half:], w[..., :half]], axis=-1)


def _prep_layer(l, w_in, w_uq, w_ukv, gmlp_w_s, gmlp_b_s, t_sample):
    wi = w_in[l]
    o_kr = Q_LORA + KV_LORA
    o_h = o_kr + QK_ROPE
    w_small = jnp.concatenate([wi[:, :o_h], _rot_cols(wi[:, o_kr:o_h])], axis=-1).astype(BF16)
    w_hgrn = wi[:, o_h:o_h + 4 * HG_WIDTH].astype(BF16)
    o_g = o_h + 4 * HG_WIDTH
    w_gu = wi[:, o_g:o_g + GM_WIDTH].astype(BF16)
    w_gv = wi[:, o_g + GM_WIDTH:o_g + 2 * GM_WIDTH].astype(BF16)
    w_gate = wi[:, o_g + 2 * GM_WIDTH:].astype(BF16)

    uq = w_uq[l].reshape(Q_LORA, MLA_HEADS, QK_NOPE + QK_ROPE)
    uq_rope = uq[..., QK_NOPE:]
    w_q = jnp.concatenate([uq[..., :QK_NOPE], uq_rope, _rot_cols(uq_rope)], axis=-1)
    w_q = w_q.reshape(Q_LORA, MLA_HEADS * Q_HEAD_PAD).astype(BF16)

    ukv = w_ukv[l].reshape(KV_LORA, MLA_HEADS, QK_NOPE + V_DIM)
    w_uk, w_uv = ukv[..., :QK_NOPE], ukv[..., QK_NOPE:]
    k_top = jnp.concatenate([w_uk, jnp.zeros((KV_LORA, MLA_HEADS, Q_HEAD_PAD - QK_NOPE), F32)], axis=-1)
    eye = jnp.concatenate([jnp.zeros((QK_ROPE, QK_NOPE), F32), jnp.eye(QK_ROPE, dtype=F32),
                           jnp.zeros((QK_ROPE, Q_HEAD_PAD - QK_NOPE - QK_ROPE), F32)], axis=-1)
    k_bot = jnp.broadcast_to(eye[:, None, :], (QK_ROPE, MLA_HEADS, Q_HEAD_PAD))
    w_k = jnp.concatenate([k_top, k_bot], axis=0).reshape(CACHE_DIM, MLA_HEADS * Q_HEAD_PAD).astype(BF16)
    w_v = jnp.concatenate([w_uv, jnp.zeros((QK_ROPE, MLA_HEADS, V_DIM), F32)], axis=0)
    w_v = w_v.reshape(CACHE_DIM, MLA_WIDTH).astype(BF16)
    w_lat = jnp.transpose(w_uk, (1, 2, 0)).astype(BF16)
    w_uvh = jnp.transpose(w_uv, (1, 0, 2)).astype(BF16)

    ws = gmlp_w_s[l]
    w_prompt = jnp.tril(ws)
    small = jnp.tril(ws[:, :t_sample, :t_sample])
    reps = GM_CHUNK // t_sample
    w_sample = jnp.einsum("ab,gts->gatbs", jnp.eye(reps, dtype=F32), small).reshape(GM_GROUPS, GM_CHUNK, GM_CHUNK)
    wmix = jnp.stack([w_prompt, w_sample]).astype(BF16)
    bs = gmlp_b_s[l]
    bias_p = jnp.repeat(bs.T[:, :, None], GM_GROUP_DIM, axis=2).reshape(GM_CHUNK, GM_WIDTH)
    bias_s = jnp.tile(jnp.repeat(bs[:, :t_sample].T[:, :, None], GM_GROUP_DIM, axis=2).reshape(t_sample, GM_WIDTH),
                      (reps, 1))
    bias = jnp.stack([bias_p, bias_s])
    return dict(w_small=w_small, w_hgrn=w_hgrn, w_gu=w_gu, w_gv=w_gv, w_gate=w_gate, w_q=w_q,
                w_kv=jnp.concatenate([w_k, w_v], axis=-1), w_lat=w_lat, w_uvh=w_uvh, wmix=wmix, bias=bias)


def _rope_tables(pos):
    half = QK_ROPE // 2
    inv = ROPE_THETA ** (-jnp.arange(half, dtype=F32) / half)
    ang = pos.astype(F32)[:, None] * inv[None, :]
    pad = jnp.zeros((pos.shape[0], LANES - QK_ROPE), F32)
    cos_t = jnp.concatenate([jnp.cos(ang), jnp.cos(ang), pad], axis=-1)
    sin_t = jnp.concatenate([jnp.sin(ang), jnp.sin(ang), pad], axis=-1)
    return cos_t, sin_t


def kernel(x_prompt, x_sample, cache_mla, state_hgrn, page_table, norm_gains, w_ffn1_gate, w_ffn1_up, w_ffn1_down, w_ffn2_gate, w_ffn2_up, w_ffn2_down, w_in, mla_q_norm, mla_kv_norm, w_uq, w_ukv, hgrn_lb_logits, hgrn_out_norm, gmlp_ln_g, gmlp_ln_b, gmlp_w_s, gmlp_b_s, w_br_mla, w_br_hgrn, w_br_gmlp, w_out):
    batch, seq, d = x_prompt.shape
    nb, t_s, _ = x_sample.shape
    m_p, m_s = batch * seq, nb * t_s
    m = m_p + m_s
    past = page_table.shape[1] * PAGE_SIZE
    tm = 1024
    while m_p % tm or m_s % tm:
        tm //= 2
    th = min(512, tm)
    tq4 = min(256, tm)
    assert tm >= GM_CHUNK and m_p % m_s == 0

    x = jnp.concatenate([x_prompt.reshape(m_p, d), x_sample.reshape(m_s, d)], axis=0)
    pos = jnp.concatenate([jnp.tile(jnp.arange(seq), batch), jnp.tile(past + jnp.arange(t_s), nb)])
    cos_t, sin_t = _rope_tables(pos)
    bf = lambda w: w.astype(BF16)
    cache_t = jnp.swapaxes(cache_mla, 2, 3)

    rows_all, s_p_all, s_s_all, vn_all = [], [], [], []
    xn = rmsnorm_cast(x, norm_gains[0, 0], tm=th)
    for l in range(DEPTH):
        ng = norm_gains[l]
        p = _prep_layer(l, w_in, w_uq, w_ukv, gmlp_w_s, gmlp_b_s, t_s)

        hmid = mm(xn, [bf(w_ffn1_gate[l]), bf(w_ffn1_up[l])], _epi_ffn_up, [(512, BF16)], tm=tm, tn=512,
                  name="ffn1_up")[0]
        x, hn = mm_res(hmid, bf(w_ffn1_down[l]), x, ng[1], ng[2], alpha=0.5, tm=tq4, name="ffn1_down")

        cqn, rows = mm(hn, [p["w_small"]], _epi_small, [(Q_LORA, BF16), (CACHE_DIM, F32)], tm=th,
                       tn=p["w_small"].shape[1], row_aux=(cos_t, sin_t),
                       col_aux=(mla_q_norm[l].reshape(1, -1), mla_kv_norm[l].reshape(1, -1)), name="in_proj_mla")
        hproj = mm(hn, [p["w_hgrn"]], _epi_identity, [(1024, F32)], tm=tm, tn=1024, name="in_proj_hgrn")[0]
        gu_act = mm(hn, [p["w_gu"]], _epi_gelu, [(GM_WIDTH, BF16)], tm=tm, tn=GM_WIDTH, name="in_proj_gu")[0]
        vn = mm(hn, [p["w_gv"]], _epi_gelu_ln, [(GM_WIDTH, F32)], tm=tm, tn=GM_WIDTH,
                col_aux=(gmlp_ln_g[l].reshape(1, -1), gmlp_ln_b[l].reshape(1, -1)), name="in_proj_gv")[0]
        gate = mm(hn, [p["w_gate"]], _epi_sigmoid, [(1024, BF16)], tm=tm, tn=1024, name="in_proj_gate")[0]

        q = mm(cqn, [p["w_q"]], _epi_q_rope, [(MLA_HEADS * Q_HEAD_PAD, BF16)], tm=th, tn=MLA_HEADS * Q_HEAD_PAD,
               row_aux=(cos_t, sin_t), name="mla_q")[0]
        kfull, vv = mm(rows, [p["w_kv"]], _epi_kv_split, [(MLA_HEADS * Q_HEAD_PAD, BF16), (MLA_WIDTH, BF16)],
                       tm=th, tn=p["w_kv"].shape[1], rows=m_p, name="mla_kv")
        a_p = mla_prompt_attention(q, kfull, vv, batch=batch, seq=seq, tq=min(512, seq), hpb=2)

        qcat = sample_query_latent(q, p["w_lat"], row0=m_p, nb=nb, t=t_s)
        o_lat = mla_sample_attention(qcat.reshape(nb, MLA_HEADS * t_s, CACHE_DIM), rows[m_p:].reshape(nb, t_s, CACHE_DIM),
                                     cache_t, page_table, layer=l, group=32)
        a_s = sample_value_proj(o_lat.reshape(nb, MLA_HEADS, t_s, KV_LORA), p["w_uvh"], nb=nb, t=t_s)
        a = jnp.concatenate([a_p, a_s], axis=0)

        b_p, s_p = hgrn_mixer(hproj, hgrn_lb_logits, hgrn_out_norm[l], None, layer=l, row0=0, nseq_total=batch,
                              seq_len=seq, blk_len=256, nseq=1, sub=HGRN_SUB, wide=HGRN_WIDE, hpb=4)
        b_s, s_s = hgrn_mixer(hproj, hgrn_lb_logits, hgrn_out_norm[l], state_hgrn[l], layer=l, row0=m_p,
                              nseq_total=nb, seq_len=t_s, blk_len=t_s, nseq=8, sub=t_s)
        bmix = jnp.concatenate([b_p, b_s], axis=0)

        cmix = gmlp_mix(gu_act, vn, p["wmix"], p["bias"], tm=th, prompt_rows=m_p)

        merged = merge_branches(a, bmix, cmix, gate, bf(w_br_mla[l]), bf(w_br_hgrn[l]), bf(w_br_gmlp[l]), tm=th)
        x, hn4 = mm_res(merged, bf(w_out[l]), x, ng[3], ng[4], alpha=1.0, tm=th, name="merge_out")

        hmid = mm(hn4, [bf(w_ffn2_gate[l]), bf(w_ffn2_up[l])], _epi_ffn_up, [(512, BF16)], tm=tm, tn=512,
                  name="ffn2_up")[0]
        g_next = norm_gains[l + 1, 0] if l + 1 < DEPTH else None
        x, xn = mm_res(hmid, bf(w_ffn2_down[l]), x, ng[5], g_next, alpha=0.5, tm=tq4, name="ffn2_down")

        rows_all.append(rows)
        s_p_all.append(s_p)
        s_s_all.append(s_s)
        vn_all.append(vn[m_p:])

    rows_st = jnp.stack(rows_all)
    return (x[:m_p].reshape(batch, seq, d), x[m_p:].reshape(nb, t_s, d),
            rows_st[:, :m_p].reshape(DEPTH, batch, seq, CACHE_DIM), rows_st[:, m_p:].reshape(DEPTH, nb, t_s, CACHE_DIM),
            jnp.stack(s_p_all), jnp.stack(s_s_all), jnp.stack(vn_all).reshape(DEPTH, nb, t_s, GM_WIDTH))
```

```python
import functools

import jax
import jax.numpy as jnp
import numpy as np
from jax import lax
from jax.experimental import pallas as pl
from jax.experimental.pallas import tpu as pltpu

D_MODEL = 2048
DEPTH = 2
PAGE_SIZE = 128
MLA_HEADS = 8
QK_NOPE = 128
QK_ROPE = 64
V_DIM = 128
Q_LORA = 512
KV_LORA = 256
ROPE_THETA = 10000.0
MLA_SCALE = (QK_NOPE + QK_ROPE) ** -0.5
MLA_WIDTH = MLA_HEADS * V_DIM
CACHE_DIM = KV_LORA + QK_ROPE
HG_HEADS = 8
HG_DK = 128
HG_DV = 128
HG_WIDTH = HG_HEADS * HG_DV
GM_GROUPS = 8
GM_GROUP_DIM = 128
GM_CHUNK = 128
GM_WIDTH = GM_GROUPS * GM_GROUP_DIM
D_FF = 5632
N_BRANCH = 3
EPS = 1e-6

LANES = 128
Q_HEAD_PAD = 2 * LANES
VMEM_CAP_BYTES = 60000 * 1024
DEC_STREAMS = 4
HGRN_SUB = 16
HGRN_WIDE = 64
HGRN_SAFE_LOG = 80.0

F32 = jnp.float32
BF16 = jnp.bfloat16


def _vmem_limit(block_bytes, temp_bytes=0):
    est = 2 * sum(block_bytes) + temp_bytes + (4 << 20)
    return int(min(max(est, 16 << 20), VMEM_CAP_BYTES))


def _nbytes(shape, dtype):
    return int(np.prod(shape)) * jnp.dtype(dtype).itemsize


def _rms(y, g):
    return y * lax.rsqrt(jnp.mean(y * y, axis=-1, keepdims=True) + EPS) * g


def _rmsnorm_kernel(xa_ref, xb_ref, g_ref, x_ref, o_ref, *, tiles_a):
    @pl.when(pl.program_id(0) < tiles_a)
    def _():
        x_ref[...] = xa_ref[...]

    @pl.when(pl.program_id(0) >= tiles_a)
    def _():
        x_ref[...] = xb_ref[...]

    o_ref[...] = _rms(x_ref[...], g_ref[...]).astype(o_ref.dtype)


def join_rmsnorm(xa, xb, g, *, tm):
    (ma, d), mb = xa.shape, xb.shape[0]
    assert ma % tm == 0 and mb % tm == 0
    ta = ma // tm
    m = ma + mb
    tile = lambda: pl.BlockSpec((tm, d), lambda i: (i, 0))
    return pl.pallas_call(
        functools.partial(_rmsnorm_kernel, tiles_a=ta),
        grid=(m // tm,),
        in_specs=[pl.BlockSpec((tm, d), lambda i: (jnp.minimum(i, ta - 1), 0)),
                  pl.BlockSpec((tm, d), lambda i: (jnp.maximum(i - ta, 0), 0)),
                  pl.BlockSpec((1, d), lambda i: (0, 0))],
        out_specs=[tile(), tile()],
        out_shape=[jax.ShapeDtypeStruct((m, d), F32), jax.ShapeDtypeStruct((m, d), BF16)],
        compiler_params=pltpu.CompilerParams(
            dimension_semantics=("arbitrary",),
            vmem_limit_bytes=_vmem_limit([3 * _nbytes((tm, d), F32), _nbytes((tm, d), BF16)], _nbytes((tm, d), F32))),
        name="join_rmsnorm",
    )(xa, xb, g.reshape(1, d))


def _mm_kernel(*refs, n_w, n_row, n_col, n_out, epi, cast_w):
    x_ref = refs[0]
    w_refs = refs[1:1 + n_w]
    row_refs = refs[1 + n_w:1 + n_w + n_row]
    col_refs = refs[1 + n_w + n_row:1 + n_w + n_row + n_col]
    out_refs = refs[1 + n_w + n_row + n_col:1 + n_w + n_row + n_col + n_out]
    if cast_w:
        wb_refs = refs[len(refs) - n_w:]

        @pl.when(pl.program_id(1) == 0)
        def _():
            for w, wb in zip(w_refs, wb_refs):
                wb[...] = w[...].astype(BF16)

        w_refs = wb_refs
    x = x_ref[...].astype(BF16)
    accs = [jnp.dot(x, w[...], preferred_element_type=F32) for w in w_refs]
    outs = epi(accs, [r[...] for r in row_refs], [c[...] for c in col_refs])
    for o_ref, o in zip(out_refs, outs):
        o_ref[...] = o.astype(o_ref.dtype)


def mm(x, ws, epi, outs, *, tm, tn, rows=None, row_aux=(), col_aux=(), layer=None, name):
    m = rows if rows is not None else x.shape[0]
    k = x.shape[1]
    n = ws[0].shape[-1]
    nt = n // tn
    cast_w = layer is not None
    assert m % tm == 0 and n % tn == 0
    in_specs = [pl.BlockSpec((tm, k), lambda j, i: (i, 0))]
    if cast_w:
        in_specs += [pl.BlockSpec((None, k, tn), lambda j, i: (layer, 0, j)) for _ in ws]
    else:
        in_specs += [pl.BlockSpec((k, tn), lambda j, i: (0, j)) for _ in ws]
    in_specs += [pl.BlockSpec((tm, a.shape[1]), lambda j, i: (i, 0)) for a in row_aux]
    in_specs += [pl.BlockSpec((1, a.shape[1] // nt), lambda j, i: (0, j)) for a in col_aux]
    out_specs = [pl.BlockSpec((tm, w), lambda j, i: (i, j)) for w, _ in outs]
    out_shape = [jax.ShapeDtypeStruct((m, w * nt), dt) for w, dt in outs]
    blocks = [_nbytes((tm, k), x.dtype)] + [_nbytes((k, tn), ws[0].dtype)] * len(ws)
    blocks += [_nbytes((tm, a.shape[1]), a.dtype) for a in row_aux]
    blocks += [_nbytes((tm, w), dt) for w, dt in outs]
    scratch = [pltpu.VMEM((k, tn), BF16) for _ in ws] if cast_w else []
    res = pl.pallas_call(
        functools.partial(_mm_kernel, n_w=len(ws), n_row=len(row_aux), n_col=len(col_aux), n_out=len(outs), epi=epi,
                          cast_w=cast_w),
        grid=(nt, m // tm),
        in_specs=in_specs,
        out_specs=out_specs,
        out_shape=out_shape,
        scratch_shapes=scratch,
        compiler_params=pltpu.CompilerParams(
            dimension_semantics=("arbitrary", "arbitrary"),
            vmem_limit_bytes=_vmem_limit(blocks, (3 * _nbytes((tm, tn), F32) + cast_w * _nbytes((k, tn), BF16))
                                         * len(ws))),
        name=name,
    )(x, *ws, *row_aux, *col_aux)
    return res


def _mm_res_kernel(*refs, alpha, has_next):
    if has_next:
        h_ref, w_ref, x_ref, gp_ref, gn_ref, xo_ref, no_ref = refs
    else:
        h_ref, w_ref, x_ref, gp_ref, xo_ref = refs
    y = jnp.dot(h_ref[...], w_ref[...], preferred_element_type=F32)
    xo = x_ref[...] + alpha * _rms(y, gp_ref[...])
    xo_ref[...] = xo
    if has_next:
        no_ref[...] = _rms(xo, gn_ref[...]).astype(no_ref.dtype)


def mm_res(h, w, x, g_post, g_next, *, alpha, tm, name):
    m, kdim = h.shape
    d = w.shape[1]
    assert m % tm == 0
    has_next = g_next is not None
    once = pl.Buffered(1)
    in_specs = [
        pl.BlockSpec((tm, kdim), lambda i: (i, 0)),
        pl.BlockSpec((kdim, d), lambda i: (0, 0), pipeline_mode=once),
        pl.BlockSpec((tm, d), lambda i: (i, 0)),
        pl.BlockSpec((1, d), lambda i: (0, 0)),
    ]
    args = [h, w, x, g_post.reshape(1, d)]
    out_specs = [pl.BlockSpec((tm, d), lambda i: (i, 0))]
    out_shape = [jax.ShapeDtypeStruct((m, d), F32)]
    streamed = [_nbytes((tm, kdim), BF16), 2 * _nbytes((tm, d), F32)]
    if has_next:
        in_specs.append(pl.BlockSpec((1, d), lambda i: (0, 0)))
        args.append(g_next.reshape(1, d))
        out_specs.append(pl.BlockSpec((tm, d), lambda i: (i, 0)))
        out_shape.append(jax.ShapeDtypeStruct((m, d), BF16))
        streamed.append(_nbytes((tm, d), BF16))
    res = pl.pallas_call(
        functools.partial(_mm_res_kernel, alpha=alpha, has_next=has_next),
        grid=(m // tm,),
        in_specs=in_specs,
        out_specs=out_specs,
        out_shape=out_shape,
        compiler_params=pltpu.CompilerParams(
            dimension_semantics=("arbitrary",),
            vmem_limit_bytes=_vmem_limit(streamed, _nbytes((kdim, d), BF16) + 3 * _nbytes((tm, d), F32))),
        name=name,
    )(*args)
    return (res[0], res[1]) if has_next else (res[0], None)


def _merge_kernel(a_ref, b_ref, c_ref, g_ref, wa_ref, wb_ref, wc_ref, o_ref, *, d):
    g = g_ref[...].astype(F32)
    acc = g[:, :d] * jnp.dot(a_ref[...], wa_ref[...], preferred_element_type=F32)
    acc += g[:, d:2 * d] * jnp.dot(b_ref[...], wb_ref[...], preferred_element_type=F32)
    acc += g[:, 2 * d:] * jnp.dot(c_ref[...], wc_ref[...], preferred_element_type=F32)
    o_ref[...] = acc.astype(o_ref.dtype)


def merge_branches(a, b, c, g, wa, wb, wc, *, tm):
    m, kw = a.shape
    d = wa.shape[1]
    row = lambda w: pl.BlockSpec((tm, w), lambda i: (i, 0))
    full = pl.BlockSpec((kw, d), lambda i: (0, 0))
    blocks = [3 * _nbytes((tm, kw), BF16), _nbytes((tm, 3 * d), BF16), 3 * _nbytes((kw, d), BF16), _nbytes((tm, d), BF16)]
    return pl.pallas_call(
        functools.partial(_merge_kernel, d=d),
        grid=(m // tm,),
        in_specs=[row(kw), row(kw), row(kw), row(3 * d), full, full, full],
        out_specs=row(d),
        out_shape=jax.ShapeDtypeStruct((m, d), BF16),
        compiler_params=pltpu.CompilerParams(
            dimension_semantics=("arbitrary",),
            vmem_limit_bytes=_vmem_limit(blocks, 4 * _nbytes((tm, d), F32))),
        name="merge_branches",
    )(a, b, c, g, wa, wb, wc)


def _flash_kernel(q_ref, k_ref, v_ref, o_ref, *, tq, scale, hpb):
    qi = pl.program_id(2)
    c2 = scale * np.log2(np.e)
    qs = [q_ref[:, h * Q_HEAD_PAD:(h + 1) * Q_HEAD_PAD] for h in range(hpb)]

    def step(kb, carry, masked):
        start = pl.multiple_of(kb * tq, tq)
        new = []
        for h in range(hpb):
            m_prev, l_prev, acc = carry[h]
            ks = k_ref[pl.ds(start, tq), h * Q_HEAD_PAD:(h + 1) * Q_HEAD_PAD]
            vs = v_ref[pl.ds(start, tq), h * V_DIM:(h + 1) * V_DIM]
            s = lax.dot_general(qs[h], ks, (((1,), (1,)), ((), ())), preferred_element_type=F32)
            if masked:
                r = lax.broadcasted_iota(jnp.int32, (tq, tq), 0)
                c = lax.broadcasted_iota(jnp.int32, (tq, tq), 1)
                s = jnp.where(c <= r, s, -jnp.inf)
            m_new = jnp.maximum(m_prev, jnp.max(s, axis=-1, keepdims=True))
            a = jnp.exp2((m_prev - m_new) * c2)
            p = jnp.exp2((s - m_new) * c2)
            l_new = a * l_prev + jnp.sum(p, axis=-1, keepdims=True)
            acc = a * acc + jnp.dot(p.astype(BF16), vs, preferred_element_type=F32)
            new.append((m_new, l_new, acc))
        return tuple(new)

    one = (jnp.full((tq, 1), -jnp.inf, F32), jnp.zeros((tq, 1), F32), jnp.zeros((tq, V_DIM), F32))
    carry = lax.fori_loop(0, qi, lambda kb, c: step(kb, c, False), (one,) * hpb)
    final = step(qi, carry, True)
    o_ref[...] = jnp.concatenate([acc / l_fin for _, l_fin, acc in final], axis=-1).astype(o_ref.dtype)


def mla_prompt_attention(q, kfull, v, *, batch, seq, tq, hpb):
    nq = seq // tq
    assert MLA_HEADS % hpb == 0
    blocks = [_nbytes((tq, hpb * Q_HEAD_PAD), BF16), _nbytes((seq, hpb * Q_HEAD_PAD), BF16),
              _nbytes((seq, hpb * V_DIM), BF16), _nbytes((tq, hpb * V_DIM), BF16)]
    return pl.pallas_call(
        functools.partial(_flash_kernel, tq=tq, scale=MLA_SCALE, hpb=hpb),
        grid=(batch, MLA_HEADS // hpb, nq),
        in_specs=[
            pl.BlockSpec((tq, hpb * Q_HEAD_PAD), lambda b, h, i: (b * nq + i, h)),
            pl.BlockSpec((seq, hpb * Q_HEAD_PAD), lambda b, h, i: (b, h)),
            pl.BlockSpec((seq, hpb * V_DIM), lambda b, h, i: (b, h)),
        ],
        out_specs=pl.BlockSpec((tq, hpb * V_DIM), lambda b, h, i: (b * nq + i, h)),
        out_shape=jax.ShapeDtypeStruct((batch * seq, MLA_WIDTH), BF16),
        compiler_params=pltpu.CompilerParams(
            dimension_semantics=("arbitrary", "arbitrary", "arbitrary"),
            vmem_limit_bytes=_vmem_limit(blocks, 6 * hpb * _nbytes((tq, tq), F32))),
        name="mla_prompt_attention",
    )(q, kfull, v)


def _dec_kernel(pt_ref, q_ref, new_ref, cache_ref, o_ref, buf, sem, *, layer, group, n_groups, t_new, scale):
    b = pl.program_id(0)
    rows_q = q_ref.shape[1]

    def page_copy(bb, g, slot, p):
        page = pt_ref[bb, g * group + p]
        dst = buf.at[slot, :, pl.ds(p * PAGE_SIZE, PAGE_SIZE)]
        return pltpu.make_async_copy(cache_ref.at[layer, page], dst, sem.at[slot])

    def start(bb, g, slot):
        for p in range(group):
            page_copy(bb, g, slot, p).start()

    def wait(bb, g, slot):
        for p in range(group):
            page_copy(bb, g, slot, p).wait()

    @pl.when(b == 0)
    def _():
        start(0, 0, 0)

    def fold(state, s, vals, vals_dim):
        m_run, l_run, acc = state
        m_new = jnp.maximum(m_run, jnp.max(s, axis=-1, keepdims=True))
        a = jnp.exp(m_run - m_new)
        p = jnp.exp(s - m_new)
        l_new = a * l_run + jnp.sum(p, axis=-1, keepdims=True)
        acc = a * acc + lax.dot_general(p.astype(BF16), vals, (((1,), (vals_dim,)), ((), ())),
                                        preferred_element_type=F32)
        return m_new, l_new, acc

    q = q_ref[0].astype(BF16)
    empty = (jnp.full((rows_q, 1), -jnp.inf, F32), jnp.zeros((rows_q, 1), F32), jnp.zeros((rows_q, KV_LORA), F32))
    states = [empty] * DEC_STREAMS
    span = group * PAGE_SIZE // DEC_STREAMS
    for g in range(n_groups):
        slot = g % 2
        if g + 1 < n_groups:
            start(b, g + 1, 1 - slot)
        else:
            @pl.when(b + 1 < pl.num_programs(0))
            def _():
                start(b + 1, 0, 1 - slot)
        wait(b, g, slot)
        for c in range(DEC_STREAMS):
            keys_t = buf[slot, :, c * span:(c + 1) * span].astype(BF16)
            s = jnp.dot(q, keys_t, preferred_element_type=F32) * scale
            states[c] = fold(states[c], s, keys_t[:KV_LORA], 1)
    keys = new_ref[0].astype(BF16)
    s = lax.dot_general(q, keys, (((1,), (1,)), ((), ())), preferred_element_type=F32) * scale
    t_of_row = lax.broadcasted_iota(jnp.int32, (rows_q, t_new), 0) % t_new
    j = lax.broadcasted_iota(jnp.int32, (rows_q, t_new), 1)
    s = jnp.where(j <= t_of_row, s, -jnp.inf)
    states[0] = fold(states[0], s, keys[:, :KV_LORA], 0)
    m_all = functools.reduce(jnp.maximum, [st[0] for st in states])
    l_fin = sum(st[1] * jnp.exp(st[0] - m_all) for st in states)
    acc = sum(st[2] * jnp.exp(st[0] - m_all) for st in states)
    o_ref[0] = acc / l_fin


def mla_sample_attention(qcat, rows_new, cache_t, page_table, *, layer, group):
    nb, rows_q, _ = qcat.shape
    t_new = rows_new.shape[1]
    n_pages = page_table.shape[1]
    n_groups = n_pages // group
    assert n_pages % group == 0 and n_groups % 2 == 0
    grid_spec = pltpu.PrefetchScalarGridSpec(
        num_scalar_prefetch=1,
        grid=(nb,),
        in_specs=[
            pl.BlockSpec((1, rows_q, CACHE_DIM), lambda b, pt: (b, 0, 0)),
            pl.BlockSpec((1, t_new, CACHE_DIM), lambda b, pt: (b, 0, 0)),
            pl.BlockSpec(memory_space=pl.ANY),
        ],
        out_specs=pl.BlockSpec((1, rows_q, KV_LORA), lambda b, pt: (b, 0, 0)),
        scratch_shapes=[
            pltpu.VMEM((2, CACHE_DIM, group * PAGE_SIZE), F32),
            pltpu.SemaphoreType.DMA((2,)),
        ],
    )
    scratch_bytes = (_nbytes((2, CACHE_DIM, group * PAGE_SIZE), F32) + _nbytes((CACHE_DIM, group * PAGE_SIZE), F32)
                     + 6 * _nbytes((rows_q, group * PAGE_SIZE), F32))
    return pl.pallas_call(
        functools.partial(_dec_kernel, layer=layer, group=group, n_groups=n_groups, t_new=t_new, scale=MLA_SCALE),
        grid_spec=grid_spec,
        out_shape=jax.ShapeDtypeStruct((nb, rows_q, KV_LORA), F32),
        compiler_params=pltpu.CompilerParams(
            dimension_semantics=("arbitrary",),
            vmem_limit_bytes=_vmem_limit([_nbytes((rows_q, CACHE_DIM), F32)], scratch_bytes)),
        name="mla_sample_attention",
    )(page_table, qcat, rows_new, cache_t)


def _qlat_kernel(q_ref, w_ref, o_ref, *, nb, t):
    q = q_ref[...]
    lat = jnp.dot(q[:, :QK_NOPE], w_ref[0], preferred_element_type=F32)
    cat = jnp.concatenate([lat, q[:, QK_NOPE:QK_NOPE + QK_ROPE].astype(F32)], axis=-1)
    o_ref[...] = cat.reshape(nb, 1, t, CACHE_DIM).astype(o_ref.dtype)


def sample_query_latent(q, w_lat, *, row0, nb, t):
    m_s = nb * t
    blk0 = row0 // m_s
    return pl.pallas_call(
        functools.partial(_qlat_kernel, nb=nb, t=t),
        grid=(MLA_HEADS,),
        in_specs=[
            pl.BlockSpec((m_s, Q_HEAD_PAD), lambda h: (blk0, h)),
            pl.BlockSpec((1, QK_NOPE, KV_LORA), lambda h: (h, 0, 0)),
        ],
        out_specs=pl.BlockSpec((nb, 1, t, CACHE_DIM), lambda h: (0, h, 0, 0)),
        out_shape=jax.ShapeDtypeStruct((nb, MLA_HEADS, t, CACHE_DIM), F32),
        compiler_params=pltpu.CompilerParams(dimension_semantics=("arbitrary",)),
        name="sample_query_latent",
    )(q, w_lat)


def _oproj_kernel(o_ref, w_ref, a_ref, *, nb, t):
    o = o_ref[...].reshape(nb * t, KV_LORA).astype(BF16)
    a_ref[...] = jnp.dot(o, w_ref[0], preferred_element_type=F32).astype(a_ref.dtype)


def sample_value_proj(o_lat, w_uv, *, nb, t):
    return pl.pallas_call(
        functools.partial(_oproj_kernel, nb=nb, t=t),
        grid=(MLA_HEADS,),
        in_specs=[
            pl.BlockSpec((nb, 1, t, KV_LORA), lambda h: (0, h, 0, 0)),
            pl.BlockSpec((1, KV_LORA, V_DIM), lambda h: (h, 0, 0)),
        ],
        out_specs=pl.BlockSpec((nb * t, V_DIM), lambda h: (0, h)),
        out_shape=jax.ShapeDtypeStruct((nb * t, MLA_WIDTH), BF16),
        compiler_params=pltpu.CompilerParams(dimension_semantics=("arbitrary",)),
        name="sample_value_proj",
    )(o_lat, w_uv)


def _hgrn_kernel(*refs, layer, sub, n_sub, nseq, has_init, wide, hpb):
    if has_init:
        lbl_ref, hq_ref, hf_ref, hi_ref, hg_ref, gain_ref, s0_ref, b_ref, sout_ref, st_ref = refs
    else:
        lbl_ref, hq_ref, hf_ref, hi_ref, hg_ref, gain_ref, b_ref, sout_ref, st_ref = refs
    rows = nseq * n_sub * sub
    logits = lbl_ref[...]
    e = jnp.exp(logits - jnp.max(logits, axis=0, keepdims=True))
    wsm = e / jnp.sum(e, axis=0, keepdims=True)
    lb = jnp.sum(wsm[:layer + 1], axis=0, keepdims=True) - wsm[0:1]

    f = lb + (1.0 - lb) * jax.nn.sigmoid(hf_ref[...])
    logf = jnp.log(f)
    kk = 1.0 - f
    q = jax.nn.silu(hq_ref[...])
    v = hi_ref[...]
    row = lax.broadcasted_iota(jnp.int32, (rows, hpb * HG_DK), 0)

    def block_cumsum(width):
        in_blk = jnp.bitwise_and(row, width - 1)
        acc = logf
        shift = 1
        while shift < width:
            acc = acc + jnp.where(in_blk >= shift, pltpu.roll(acc, shift, 0), 0.0)
            shift *= 2
        return acc

    def lower_tri(width):
        return (lax.broadcasted_iota(jnp.int32, (width, width), 1)
                <= lax.broadcasted_iota(jnp.int32, (width, width), 0))

    def head(x, h):
        return x[:, h * HG_DK:(h + 1) * HG_DK]

    def exact_blocks(sts, lo, hi):
        bcum = block_cumsum(sub)
        qt = q * jnp.exp(bcum)
        tri = lower_tri(sub)
        sts = list(sts)
        outs = [[] for _ in range(hpb)]
        for r0 in range(lo, hi, sub):
            sl = slice(r0, r0 + sub)
            for h in range(hpb):
                bj, qj, kj = head(bcum[sl], h), head(q[sl], h), head(kk[sl], h)
                vj = head(v[sl], h).astype(BF16)
                o_inter = lax.dot_general(head(qt[sl], h).astype(BF16), sts[h].astype(BF16),
                                          (((1,), (1,)), ((), ())), preferred_element_type=F32)
                diff = jnp.minimum(bj[:, None, :] - bj[None, :, :], 0.0)
                att = jnp.sum(qj[:, None, :] * kj[None, :, :] * jnp.exp(diff), axis=-1)
                att = jnp.where(tri, att, 0.0)
                o_intra = jnp.dot(att.astype(BF16), vj, preferred_element_type=F32)
                bl = bj[sub - 1:sub, :]
                kt = (kj * jnp.exp(bl - bj)).astype(BF16)
                upd = lax.dot_general(vj, kt, (((0,), (0,)), ((), ())), preferred_element_type=F32)
                sts[h] = sts[h] * jnp.exp(bl) + upd
                outs[h].append(o_intra + o_inter)
        return tuple(jnp.concatenate(o_h, axis=0) for o_h in outs), tuple(sts)

    def factored_blocks(sts, bcum, lo, hi):
        qt = q * jnp.exp(bcum)
        kt = kk * jnp.exp(-bcum)
        tri = lower_tri(wide)
        sts = list(sts)
        outs = [[] for _ in range(hpb)]
        for r0 in range(lo, hi, wide):
            sl = slice(r0, r0 + wide)
            bl_all = bcum[r0 + wide - 1:r0 + wide, :]
            kl_all = kk[sl] * jnp.exp(bl_all - bcum[sl])
            for h in range(hpb):
                qj, vj = head(qt[sl], h).astype(BF16), head(v[sl], h).astype(BF16)
                att = lax.dot_general(qj, head(kt[sl], h).astype(BF16), (((1,), (1,)), ((), ())),
                                      preferred_element_type=F32)
                att = jnp.where(tri, att, 0.0).astype(BF16)
                o_blk = jnp.dot(att, vj, preferred_element_type=F32)
                o_blk += lax.dot_general(qj, sts[h].astype(BF16), (((1,), (1,)), ((), ())),
                                         preferred_element_type=F32)
                upd = lax.dot_general(vj, head(kl_all, h).astype(BF16), (((0,), (0,)), ((), ())),
                                      preferred_element_type=F32)
                sts[h] = sts[h] * jnp.exp(head(bl_all, h)) + upd
                outs[h].append(o_blk)
        return tuple(jnp.concatenate(o_h, axis=0) for o_h in outs), tuple(sts)

    if not has_init:
        @pl.when(pl.program_id(2) == 0)
        def _():
            st_ref[...] = jnp.zeros_like(st_ref)

    per_seq = n_sub * sub
    outs = []
    for s in range(nseq):
        st0 = tuple(s0_ref[s, h].T if has_init else st_ref[h] for h in range(hpb))
        lo, hi = s * per_seq, (s + 1) * per_seq
        if wide is None:
            o_s, sts = exact_blocks(st0, lo, hi)
        else:
            bwide = block_cumsum(wide)
            safe = jnp.min(bwide) >= -HGRN_SAFE_LOG
            o_s, sts = lax.cond(safe, lambda s_in: factored_blocks(s_in, bwide, lo, hi),
                                lambda s_in: exact_blocks(s_in, lo, hi), st0)
        outs.append(o_s)
        for h in range(hpb):
            if has_init:
                sout_ref[s, h] = sts[h].T
            else:
                st_ref[h] = sts[h]
        if not has_init:
            @pl.when(pl.program_id(2) == pl.num_programs(2) - 1)
            def _():
                for h in range(hpb):
                    sout_ref[0, h] = sts[h].T
    gain = gain_ref[...]
    normed = []
    for h in range(hpb):
        o_h = outs[0][h] if nseq == 1 else jnp.concatenate([o_s[h] for o_s in outs], axis=0)
        normed.append(_rms(o_h, head(gain, h)))
    o = normed[0] if hpb == 1 else jnp.concatenate(normed, axis=-1)
    b_ref[...] = (o * jax.nn.silu(hg_ref[...])).astype(b_ref.dtype)


def hgrn_mixer(hproj, lb_logits, gain, state0, *, layer, row0, nseq_total, seq_len, blk_len, nseq, sub, wide=None,
               hpb=1):
    has_init = state0 is not None
    n_sub = blk_len // sub
    nblk = seq_len // blk_len
    rows = nseq * blk_len
    assert row0 % rows == 0 and nseq_total % nseq == 0 and (nseq == 1 or nblk == 1)
    r0 = row0 // rows
    assert HG_HEADS % hpb == 0
    hg = HG_HEADS // hpb
    wcol = hpb * HG_DK
    col = lambda p: pl.BlockSpec((rows, wcol), lambda s, h, c: (r0 + s * nblk + c, p * hg + h))
    in_specs = [
        pl.BlockSpec((DEPTH, wcol), lambda s, h, c: (0, h)),
        col(0), col(1), col(2), col(3),
        pl.BlockSpec((1, wcol), lambda s, h, c: (0, h)),
    ]
    args = [lb_logits, hproj, hproj, hproj, hproj, gain.reshape(1, HG_WIDTH)]
    if has_init:
        in_specs.append(pl.BlockSpec((None, nseq, hpb, HG_DK, HG_DV), lambda s, h, c: (layer, s, h, 0, 0)))
        args.append(state0)
    out_specs = [
        pl.BlockSpec((rows, wcol), lambda s, h, c: (s * nblk + c, h)),
        pl.BlockSpec((nseq, hpb, HG_DK, HG_DV), lambda s, h, c: (s, h, 0, 0)),
    ]
    out_shape = [
        jax.ShapeDtypeStruct((nseq_total * seq_len, HG_WIDTH), BF16),
        jax.ShapeDtypeStruct((nseq_total, HG_HEADS, HG_DK, HG_DV), F32),
    ]
    return pl.pallas_call(
        functools.partial(_hgrn_kernel, layer=layer, sub=sub, n_sub=n_sub, nseq=nseq, has_init=has_init, wide=wide,
                          hpb=hpb),
        grid=(nseq_total // nseq, hg, nblk),
        in_specs=in_specs,
        out_specs=out_specs,
        out_shape=out_shape,
        scratch_shapes=[pltpu.VMEM((hpb, HG_DV, HG_DK), F32)],
        compiler_params=pltpu.CompilerParams(
            dimension_semantics=("arbitrary", "arbitrary", "arbitrary"),
            vmem_limit_bytes=32 << 20),
        name="hgrn_prompt" if not has_init else "hgrn_sample",
    )(*args)


def _gmlp_kernel(gu_ref, vn_ref, w_ref, bias_ref, o_ref, *, n_chunks):
    for c in range(n_chunks):
        rs = slice(c * GM_CHUNK, (c + 1) * GM_CHUNK)
        vn = vn_ref[rs, :].astype(BF16)
        parts = []
        for g in range(GM_GROUPS):
            cs = slice(g * GM_GROUP_DIM, (g + 1) * GM_GROUP_DIM)
            parts.append(jnp.dot(w_ref[0, g], vn[:, cs], preferred_element_type=F32))
        mixed = jnp.concatenate(parts, axis=-1) + bias_ref[0]
        o_ref[rs, :] = (gu_ref[rs, :].astype(F32) * mixed).astype(o_ref.dtype)


def gmlp_mix(gu_act, vn, wmix, bias, *, tm, prompt_rows):
    m = gu_act.shape[0]
    first_sample_tile = prompt_rows // tm
    kind = lambda i: jnp.where(i >= first_sample_tile, 1, 0)
    return pl.pallas_call(
        functools.partial(_gmlp_kernel, n_chunks=tm // GM_CHUNK),
        grid=(m // tm,),
        in_specs=[
            pl.BlockSpec((tm, GM_WIDTH), lambda i: (i, 0)),
            pl.BlockSpec((tm, GM_WIDTH), lambda i: (i, 0)),
            pl.BlockSpec((1, GM_GROUPS, GM_CHUNK, GM_CHUNK), lambda i: (kind(i), 0, 0, 0)),
            pl.BlockSpec((1, GM_CHUNK, GM_WIDTH), lambda i: (kind(i), 0, 0)),
        ],
        out_specs=pl.BlockSpec((tm, GM_WIDTH), lambda i: (i, 0)),
        out_shape=jax.ShapeDtypeStruct((m, GM_WIDTH), BF16),
        compiler_params=pltpu.CompilerParams(dimension_semantics=("arbitrary",), vmem_limit_bytes=32 << 20),
        name="gmlp_mix",
    )(gu_act, vn, wmix, bias)


def _rope_tile(t, cos_t, sin_t):
    return t * cos_t + pltpu.roll(t, QK_ROPE, 1) * sin_t


def _epi_ffn_up(accs, row_aux, col_aux):
    return [jax.nn.silu(accs[0]) * accs[1]]


def _epi_small(accs, row_aux, col_aux):
    y = accs[0]
    cos_t, sin_t = row_aux
    qg, kvg = col_aux
    cqn = _rms(y[:, :Q_LORA], qg)
    ckvn = _rms(y[:, Q_LORA:Q_LORA + KV_LORA], kvg)
    roped = _rope_tile(y[:, Q_LORA + KV_LORA:], cos_t, sin_t)
    return [cqn, jnp.concatenate([ckvn, roped[:, :QK_ROPE]], axis=-1)]


def _epi_identity(accs, row_aux, col_aux):
    return [accs[0]]


def _epi_gelu(accs, row_aux, col_aux):
    return [jax.nn.gelu(accs[0])]


def _epi_gelu_ln(accs, row_aux, col_aux):
    y = jax.nn.gelu(accs[0])
    g, b = col_aux
    mu = jnp.mean(y, axis=-1, keepdims=True)
    var = jnp.mean(jnp.square(y - mu), axis=-1, keepdims=True)
    return [(y - mu) * lax.rsqrt(var + EPS) * g + b]


def _epi_sigmoid(accs, row_aux, col_aux):
    return [jax.nn.sigmoid(accs[0])]


def _epi_q_rope(accs, row_aux, col_aux):
    y = accs[0]
    cos_t, sin_t = row_aux
    parts = []
    for h in range(MLA_HEADS):
        base = h * Q_HEAD_PAD
        parts.append(y[:, base:base + QK_NOPE])
        parts.append(_rope_tile(y[:, base + QK_NOPE:base + Q_HEAD_PAD], cos_t, sin_t))
    return [jnp.concatenate(parts, axis=-1)]


def _epi_kv_split(accs, row_aux, col_aux):
    n_k = MLA_HEADS * Q_HEAD_PAD
    return [accs[0][:, :n_k], accs[0][:, n_k:]]


def _rot_cols(w):
    half = QK_ROPE // 2
    return jnp.concatenate([-w[..., half:], w[..., :half]], axis=-1)


def _prep_layer(l, w_in, w_uq, w_ukv, gmlp_w_s, gmlp_b_s, t_sample):
    wi = w_in[l]
    o_kr = Q_LORA + KV_LORA
    o_h = o_kr + QK_ROPE
    w_small = jnp.concatenate([wi[:, :o_h], _rot_cols(wi[:, o_kr:o_h])], axis=-1).astype(BF16)
    w_hgrn = wi[:, o_h:o_h + 4 * HG_WIDTH].astype(BF16)
    o_g = o_h + 4 * HG_WIDTH
    w_gu = wi[:, o_g:o_g + GM_WIDTH].astype(BF16)
    w_gv = wi[:, o_g + GM_WIDTH:o_g + 2 * GM_WIDTH].astype(BF16)
    w_gate = wi[:, o_g + 2 * GM_WIDTH:].astype(BF16)

    uq = w_uq[l].reshape(Q_LORA, MLA_HEADS, QK_NOPE + QK_ROPE)
    uq_rope = uq[..., QK_NOPE:]
    w_q = jnp.concatenate([uq[..., :QK_NOPE], uq_rope, _rot_cols(uq_rope)], axis=-1)
    w_q = w_q.reshape(Q_LORA, MLA_HEADS * Q_HEAD_PAD).astype(BF16)

    ukv = w_ukv[l].reshape(KV_LORA, MLA_HEADS, QK_NOPE + V_DIM)
    w_uk, w_uv = ukv[..., :QK_NOPE], ukv[..., QK_NOPE:]
    k_top = jnp.concatenate([w_uk, jnp.zeros((KV_LORA, MLA_HEADS, Q_HEAD_PAD - QK_NOPE), F32)], axis=-1)
    eye = jnp.concatenate([jnp.zeros((QK_ROPE, QK_NOPE), F32), jnp.eye(QK_ROPE, dtype=F32),
                           jnp.zeros((QK_ROPE, Q_HEAD_PAD - QK_NOPE - QK_ROPE), F32)], axis=-1)
    k_bot = jnp.broadcast_to(eye[:, None, :], (QK_ROPE, MLA_HEADS, Q_HEAD_PAD))
    w_k = jnp.concatenate([k_top, k_bot], axis=0).reshape(CACHE_DIM, MLA_HEADS * Q_HEAD_PAD).astype(BF16)
    w_v = jnp.concatenate([w_uv, jnp.zeros((QK_ROPE, MLA_HEADS, V_DIM), F32)], axis=0)
    w_v = w_v.reshape(CACHE_DIM, MLA_WIDTH).astype(BF16)
    w_lat = jnp.transpose(w_uk, (1, 2, 0)).astype(BF16)
    w_uvh = jnp.transpose(w_uv, (1, 0, 2)).astype(BF16)

    ws = gmlp_w_s[l]
    w_prompt = jnp.tril(ws)
    small = jnp.tril(ws[:, :t_sample, :t_sample])
    reps = GM_CHUNK // t_sample
    w_sample = jnp.einsum("ab,gts->gatbs", jnp.eye(reps, dtype=F32), small).reshape(GM_GROUPS, GM_CHUNK, GM_CHUNK)
    wmix = jnp.stack([w_prompt, w_sample]).astype(BF16)
    bs = gmlp_b_s[l]
    bias_p = jnp.repeat(bs.T[:, :, None], GM_GROUP_DIM, axis=2).reshape(GM_CHUNK, GM_WIDTH)
    bias_s = jnp.tile(jnp.repeat(bs[:, :t_sample].T[:, :, None], GM_GROUP_DIM, axis=2).reshape(t_sample, GM_WIDTH),
                      (reps, 1))
    bias = jnp.stack([bias_p, bias_s])
    return dict(w_small=w_small, w_hgrn=w_hgrn, w_gu=w_gu, w_gv=w_gv, w_gate=w_gate, w_q=w_q,
                w_kv=jnp.concatenate([w_k, w_v], axis=-1), w_lat=w_lat, w_uvh=w_uvh, wmix=wmix, bias=bias)


def _rope_tables(pos):
    half = QK_ROPE // 2
    inv = ROPE_THETA ** (-jnp.arange(half, dtype=F32) / half)
    ang = pos.astype(F32)[:, None] * inv[None, :]
    pad = jnp.zeros((pos.shape[0], LANES - QK_ROPE), F32)
    cos_t = jnp.concatenate([jnp.cos(ang), jnp.cos(ang), pad], axis=-1)
    sin_t = jnp.concatenate([jnp.sin(ang), jnp.sin(ang), pad], axis=-1)
    return cos_t, sin_t


def kernel(x_prompt, x_sample, cache_mla, state_hgrn, page_table, norm_gains, w_ffn1_gate, w_ffn1_up, w_ffn1_down, w_ffn2_gate, w_ffn2_up, w_ffn2_down, w_in, mla_q_norm, mla_kv_norm, w_uq, w_ukv, hgrn_lb_logits, hgrn_out_norm, gmlp_ln_g, gmlp_ln_b, gmlp_w_s, gmlp_b_s, w_br_mla, w_br_hgrn, w_br_gmlp, w_out):
    batch, seq, d = x_prompt.shape
    nb, t_s, _ = x_sample.shape
    m_p, m_s = batch * seq, nb * t_s
    m = m_p + m_s
    past = page_table.shape[1] * PAGE_SIZE
    tm = 1024
    while m_p % tm or m_s % tm:
        tm //= 2
    th = min(512, tm)
    tq4 = min(256, tm)
    assert tm >= GM_CHUNK and m_p % m_s == 0

    pos = jnp.concatenate([jnp.tile(jnp.arange(seq), batch), jnp.tile(past + jnp.arange(t_s), nb)])
    cos_t, sin_t = _rope_tables(pos)
    bf = lambda w: w.astype(BF16)
    cache_t = jnp.swapaxes(cache_mla, 2, 3)

    rows_all, s_p_all, s_s_all, vn_all = [], [], [], []
    x, xn = join_rmsnorm(x_prompt.reshape(m_p, d), x_sample.reshape(m_s, d), norm_gains[0, 0], tm=th)
    for l in range(DEPTH):
        ng = norm_gains[l]
        p = _prep_layer(l, w_in, w_uq, w_ukv, gmlp_w_s, gmlp_b_s, t_s)

        hmid = mm(xn, [w_ffn1_gate, w_ffn1_up], _epi_ffn_up, [(512, BF16)], tm=tm, tn=512, layer=l,
                  name="ffn1_up")[0]
        x, hn = mm_res(hmid, bf(w_ffn1_down[l]), x, ng[1], ng[2], alpha=0.5, tm=tq4, name="ffn1_down")

        cqn, rows = mm(hn, [p["w_small"]], _epi_small, [(Q_LORA, BF16), (CACHE_DIM, F32)], tm=th,
                       tn=p["w_small"].shape[1], row_aux=(cos_t, sin_t),
                       col_aux=(mla_q_norm[l].reshape(1, -1), mla_kv_norm[l].reshape(1, -1)), name="in_proj_mla")
        hproj = mm(hn, [p["w_hgrn"]], _epi_identity, [(1024, F32)], tm=tm, tn=1024, name="in_proj_hgrn")[0]
        gu_act = mm(hn, [p["w_gu"]], _epi_gelu, [(GM_WIDTH, BF16)], tm=tm, tn=GM_WIDTH, name="in_proj_gu")[0]
        vn = mm(hn, [p["w_gv"]], _epi_gelu_ln, [(GM_WIDTH, F32)], tm=tm, tn=GM_WIDTH,
                col_aux=(gmlp_ln_g[l].reshape(1, -1), gmlp_ln_b[l].reshape(1, -1)), name="in_proj_gv")[0]
        gate = mm(hn, [p["w_gate"]], _epi_sigmoid, [(1024, BF16)], tm=tm, tn=1024, name="in_proj_gate")[0]

        q = mm(cqn, [p["w_q"]], _epi_q_rope, [(MLA_HEADS * Q_HEAD_PAD, BF16)], tm=th, tn=MLA_HEADS * Q_HEAD_PAD,
               row_aux=(cos_t, sin_t), name="mla_q")[0]
        kfull, vv = mm(rows, [p["w_kv"]], _epi_kv_split, [(MLA_HEADS * Q_HEAD_PAD, BF16), (MLA_WIDTH, BF16)],
                       tm=th, tn=p["w_kv"].shape[1], rows=m_p, name="mla_kv")
        a_p = mla_prompt_attention(q, kfull, vv, batch=batch, seq=seq, tq=min(1024, seq), hpb=2)

        qcat = sample_query_latent(q, p["w_lat"], row0=m_p, nb=nb, t=t_s)
        o_lat = mla_sample_attention(qcat.reshape(nb, MLA_HEADS * t_s, CACHE_DIM), rows[m_p:].reshape(nb, t_s, CACHE_DIM),
                                     cache_t, page_table, layer=l, group=32)
        a_s = sample_value_proj(o_lat.reshape(nb, MLA_HEADS, t_s, KV_LORA), p["w_uvh"], nb=nb, t=t_s)
        a = jnp.concatenate([a_p, a_s], axis=0)

        b_p, s_p = hgrn_mixer(hproj, hgrn_lb_logits, hgrn_out_norm[l], None, layer=l, row0=0, nseq_total=batch,
                              seq_len=seq, blk_len=256, nseq=1, sub=HGRN_SUB, wide=HGRN_WIDE, hpb=4)
        b_s, s_s = hgrn_mixer(hproj, hgrn_lb_logits, hgrn_out_norm[l], state_hgrn, layer=l, row0=m_p,
                              nseq_total=nb, seq_len=t_s, blk_len=t_s, nseq=8, sub=t_s, hpb=4)
        bmix = jnp.concatenate([b_p, b_s], axis=0)

        cmix = gmlp_mix(gu_act, vn, p["wmix"], p["bias"], tm=th, prompt_rows=m_p)

        merged = merge_branches(a, bmix, cmix, gate, bf(w_br_mla[l]), bf(w_br_hgrn[l]), bf(w_br_gmlp[l]), tm=th)
        x, hn4 = mm_res(merged, bf(w_out[l]), x, ng[3], ng[4], alpha=1.0, tm=th, name="merge_out")

        hmid = mm(hn4, [w_ffn2_gate, w_ffn2_up], _epi_ffn_up, [(512, BF16)], tm=tm, tn=512, layer=l,
                  name="ffn2_up")[0]
        g_next = norm_gains[l + 1, 0] if l + 1 < DEPTH else None
        x, xn = mm_res(hmid, bf(w_ffn2_down[l]), x, ng[5], g_next, alpha=0.5, tm=tq4, name="ffn2_down")

        rows_all.append(rows)
        s_p_all.append(s_p)
        s_s_all.append(s_s)
        vn_all.append(vn[m_p:])

    rows_st = jnp.stack(rows_all)
    return (x[:m_p].reshape(batch, seq, d), x[m_p:].reshape(nb, t_s, d),
            rows_st[:, :m_p].reshape(DEPTH, batch, seq, CACHE_DIM), rows_st[:, m_p:].reshape(DEPTH, nb, t_s, CACHE_DIM),
            jnp.stack(s_p_all), jnp.stack(s_s_all), jnp.stack(vn_all).reshape(DEPTH, nb, t_s, GM_WIDTH))
```

```python
import functools

import jax
import jax.numpy as jnp
import numpy as np
from jax import lax
from jax.experimental import pallas as pl
from jax.experimental.pallas import tpu as pltpu

D_MODEL = 2048
DEPTH = 2
PAGE_SIZE = 128
MLA_HEADS = 8
QK_NOPE = 128
QK_ROPE = 64
V_DIM = 128
Q_LORA = 512
KV_LORA = 256
ROPE_THETA = 10000.0
MLA_SCALE = (QK_NOPE + QK_ROPE) ** -0.5
MLA_WIDTH = MLA_HEADS * V_DIM
CACHE_DIM = KV_LORA + QK_ROPE
HG_HEADS = 8
HG_DK = 128
HG_DV = 128
HG_WIDTH = HG_HEADS * HG_DV
GM_GROUPS = 8
GM_GROUP_DIM = 128
GM_CHUNK = 128
GM_WIDTH = GM_GROUPS * GM_GROUP_DIM
D_FF = 5632
N_BRANCH = 3
EPS = 1e-6

LANES = 128
Q_HEAD_PAD = 2 * LANES
VMEM_CAP_BYTES = 60000 * 1024
DEC_STREAMS = 4
DEC_AHEAD = 2
HGRN_SUB = 16
HGRN_WIDE = 64
HGRN_SAFE_LOG = 80.0

F32 = jnp.float32
BF16 = jnp.bfloat16


def _vmem_limit(block_bytes, temp_bytes=0):
    est = 2 * sum(block_bytes) + temp_bytes + (4 << 20)
    return int(min(max(est, 16 << 20), VMEM_CAP_BYTES))


def _nbytes(shape, dtype):
    return int(np.prod(shape)) * jnp.dtype(dtype).itemsize


def _rms(y, g):
    return y * lax.rsqrt(jnp.mean(y * y, axis=-1, keepdims=True) + EPS) * g


def _rmsnorm_kernel(xa_ref, xb_ref, g_ref, x_ref, o_ref, *, tiles_a):
    @pl.when(pl.program_id(0) < tiles_a)
    def _():
        x_ref[...] = xa_ref[...]

    @pl.when(pl.program_id(0) >= tiles_a)
    def _():
        x_ref[...] = xb_ref[...]

    o_ref[...] = _rms(x_ref[...], g_ref[...]).astype(o_ref.dtype)


def join_rmsnorm(xa, xb, g, *, tm):
    (ma, d), mb = xa.shape, xb.shape[0]
    assert ma % tm == 0 and mb % tm == 0
    ta = ma // tm
    m = ma + mb
    tile = lambda: pl.BlockSpec((tm, d), lambda i: (i, 0))
    return pl.pallas_call(
        functools.partial(_rmsnorm_kernel, tiles_a=ta),
        grid=(m // tm,),
        in_specs=[pl.BlockSpec((tm, d), lambda i: (jnp.minimum(i, ta - 1), 0)),
                  pl.BlockSpec((tm, d), lambda i: (jnp.maximum(i - ta, 0), 0)),
                  pl.BlockSpec((1, d), lambda i: (0, 0))],
        out_specs=[tile(), tile()],
        out_shape=[jax.ShapeDtypeStruct((m, d), F32), jax.ShapeDtypeStruct((m, d), BF16)],
        compiler_params=pltpu.CompilerParams(
            dimension_semantics=("arbitrary",),
            vmem_limit_bytes=_vmem_limit([3 * _nbytes((tm, d), F32), _nbytes((tm, d), BF16)], _nbytes((tm, d), F32))),
        name="join_rmsnorm",
    )(xa, xb, g.reshape(1, d))


def _mm_kernel(*refs, n_w, n_row, n_col, n_out, epi, cast_w):
    x_ref = refs[0]
    w_refs = refs[1:1 + n_w]
    row_refs = refs[1 + n_w:1 + n_w + n_row]
    col_refs = refs[1 + n_w + n_row:1 + n_w + n_row + n_col]
    out_refs = refs[1 + n_w + n_row + n_col:1 + n_w + n_row + n_col + n_out]
    if cast_w:
        wb_refs = refs[len(refs) - n_w:]

        @pl.when(pl.program_id(1) == 0)
        def _():
            for w, wb in zip(w_refs, wb_refs):
                wb[...] = w[...].astype(BF16)

        w_refs = wb_refs
    x = x_ref[...].astype(BF16)
    accs = [jnp.dot(x, w[...], preferred_element_type=F32) for w in w_refs]
    outs = epi(accs, [r[...] for r in row_refs], [c[...] for c in col_refs])
    for o_ref, o in zip(out_refs, outs):
        o_ref[...] = o.astype(o_ref.dtype)


def mm(x, ws, epi, outs, *, tm, tn, rows=None, row_aux=(), col_aux=(), layer=None, name):
    m = rows if rows is not None else x.shape[0]
    k = x.shape[1]
    n = ws[0].shape[-1]
    nt = n // tn
    cast_w = layer is not None
    assert m % tm == 0 and n % tn == 0
    in_specs = [pl.BlockSpec((tm, k), lambda j, i: (i, 0))]
    if cast_w:
        in_specs += [pl.BlockSpec((None, k, tn), lambda j, i: (layer, 0, j)) for _ in ws]
    else:
        in_specs += [pl.BlockSpec((k, tn), lambda j, i: (0, j)) for _ in ws]
    in_specs += [pl.BlockSpec((tm, a.shape[1]), lambda j, i: (i, 0)) for a in row_aux]
    in_specs += [pl.BlockSpec((1, a.shape[1] // nt), lambda j, i: (0, j)) for a in col_aux]
    out_specs = [pl.BlockSpec((tm, w), lambda j, i: (i, j)) for w, _ in outs]
    out_shape = [jax.ShapeDtypeStruct((m, w * nt), dt) for w, dt in outs]
    blocks = [_nbytes((tm, k), x.dtype)] + [_nbytes((k, tn), ws[0].dtype)] * len(ws)
    blocks += [_nbytes((tm, a.shape[1]), a.dtype) for a in row_aux]
    blocks += [_nbytes((tm, w), dt) for w, dt in outs]
    scratch = [pltpu.VMEM((k, tn), BF16) for _ in ws] if cast_w else []
    res = pl.pallas_call(
        functools.partial(_mm_kernel, n_w=len(ws), n_row=len(row_aux), n_col=len(col_aux), n_out=len(outs), epi=epi,
                          cast_w=cast_w),
        grid=(nt, m // tm),
        in_specs=in_specs,
        out_specs=out_specs,
        out_shape=out_shape,
        scratch_shapes=scratch,
        compiler_params=pltpu.CompilerParams(
            dimension_semantics=("arbitrary", "arbitrary"),
            vmem_limit_bytes=_vmem_limit(blocks, (3 * _nbytes((tm, tn), F32) + cast_w * _nbytes((k, tn), BF16))
                                         * len(ws))),
        name=name,
    )(x, *ws, *row_aux, *col_aux)
    return res


def _mm_res_kernel(*refs, alpha, has_next):
    if has_next:
        h_ref, w_ref, x_ref, gp_ref, gn_ref, xo_ref, no_ref, y_ref = refs
    else:
        h_ref, w_ref, x_ref, gp_ref, xo_ref, y_ref = refs
    i = pl.program_id(0)
    slot = lax.rem(i, 2)

    @pl.when(i == 0)
    def _():
        y_ref[1] = jnp.zeros(y_ref.shape[1:], F32)

    def step(cur, prev):
        y_ref[cur] = jnp.dot(h_ref[...], w_ref[...], preferred_element_type=F32)
        xo = x_ref[...] + alpha * _rms(y_ref[prev], gp_ref[...])
        xo_ref[...] = xo
        if has_next:
            no_ref[...] = _rms(xo, gn_ref[...]).astype(no_ref.dtype)

    for parity in (0, 1):
        pl.when(slot == parity)(functools.partial(step, parity, 1 - parity))


def mm_res(h, w, x, g_post, g_next, *, alpha, tm, name):
    m, kdim = h.shape
    d = w.shape[1]
    assert m % tm == 0
    n_tiles = m // tm
    has_next = g_next is not None
    once = pl.Buffered(1)
    lagged = lambda: pl.BlockSpec((tm, d), lambda i: (jnp.maximum(i - 1, 0), 0))
    in_specs = [
        pl.BlockSpec((tm, kdim), lambda i: (jnp.minimum(i, n_tiles - 1), 0)),
        pl.BlockSpec((kdim, d), lambda i: (0, 0), pipeline_mode=once),
        lagged(),
        pl.BlockSpec((1, d), lambda i: (0, 0)),
    ]
    args = [h, w, x, g_post.reshape(1, d)]
    out_specs = [lagged()]
    out_shape = [jax.ShapeDtypeStruct((m, d), F32)]
    streamed = [_nbytes((tm, kdim), BF16), 2 * _nbytes((tm, d), F32)]
    if has_next:
        in_specs.append(pl.BlockSpec((1, d), lambda i: (0, 0)))
        args.append(g_next.reshape(1, d))
        out_specs.append(lagged())
        out_shape.append(jax.ShapeDtypeStruct((m, d), BF16))
        streamed.append(_nbytes((tm, d), BF16))
    res = pl.pallas_call(
        functools.partial(_mm_res_kernel, alpha=alpha, has_next=has_next),
        grid=(n_tiles + 1,),
        in_specs=in_specs,
        out_specs=out_specs,
        out_shape=out_shape,
        scratch_shapes=[pltpu.VMEM((2, tm, d), F32)],
        compiler_params=pltpu.CompilerParams(
            dimension_semantics=("arbitrary",),
            vmem_limit_bytes=_vmem_limit(streamed, _nbytes((kdim, d), BF16) + 5 * _nbytes((tm, d), F32))),
        name=name,
    )(*args)
    return (res[0], res[1]) if has_next else (res[0], None)


def _merge_kernel(a_ref, b_ref, c_ref, g_ref, wa_ref, wb_ref, wc_ref, o_ref, *, d):
    g = g_ref[...].astype(F32)
    acc = g[:, :d] * jnp.dot(a_ref[...], wa_ref[...], preferred_element_type=F32)
    acc += g[:, d:2 * d] * jnp.dot(b_ref[...], wb_ref[...], preferred_element_type=F32)
    acc += g[:, 2 * d:] * jnp.dot(c_ref[...], wc_ref[...], preferred_element_type=F32)
    o_ref[...] = acc.astype(o_ref.dtype)


def merge_branches(a, b, c, g, wa, wb, wc, *, tm):
    m, kw = a.shape
    d = wa.shape[1]
    row = lambda w: pl.BlockSpec((tm, w), lambda i: (i, 0))
    full = pl.BlockSpec((kw, d), lambda i: (0, 0))
    blocks = [3 * _nbytes((tm, kw), BF16), _nbytes((tm, 3 * d), BF16), 3 * _nbytes((kw, d), BF16), _nbytes((tm, d), BF16)]
    return pl.pallas_call(
        functools.partial(_merge_kernel, d=d),
        grid=(m // tm,),
        in_specs=[row(kw), row(kw), row(kw), row(3 * d), full, full, full],
        out_specs=row(d),
        out_shape=jax.ShapeDtypeStruct((m, d), BF16),
        compiler_params=pltpu.CompilerParams(
            dimension_semantics=("arbitrary",),
            vmem_limit_bytes=_vmem_limit(blocks, 4 * _nbytes((tm, d), F32))),
        name="merge_branches",
    )(a, b, c, g, wa, wb, wc)


def _flash_kernel(q_ref, k_ref, v_ref, o_ref, *, tq, scale, hpb):
    qi = pl.program_id(2)
    c2 = scale * np.log2(np.e)
    qs = [q_ref[:, h * Q_HEAD_PAD:(h + 1) * Q_HEAD_PAD] for h in range(hpb)]

    def step(kb, carry, masked):
        start = pl.multiple_of(kb * tq, tq)
        new = []
        for h in range(hpb):
            m_prev, l_prev, acc = carry[h]
            ks = k_ref[pl.ds(start, tq), h * Q_HEAD_PAD:(h + 1) * Q_HEAD_PAD]
            vs = v_ref[pl.ds(start, tq), h * V_DIM:(h + 1) * V_DIM]
            s = lax.dot_general(qs[h], ks, (((1,), (1,)), ((), ())), preferred_element_type=F32)
            if masked:
                r = lax.broadcasted_iota(jnp.int32, (tq, tq), 0)
                c = lax.broadcasted_iota(jnp.int32, (tq, tq), 1)
                s = jnp.where(c <= r, s, -jnp.inf)
            m_new = jnp.maximum(m_prev, jnp.max(s, axis=-1, keepdims=True))
            a = jnp.exp2((m_prev - m_new) * c2)
            p = jnp.exp2((s - m_new) * c2)
            l_new = a * l_prev + jnp.sum(p, axis=-1, keepdims=True)
            acc = a * acc + jnp.dot(p.astype(BF16), vs, preferred_element_type=F32)
            new.append((m_new, l_new, acc))
        return tuple(new)

    one = (jnp.full((tq, 1), -jnp.inf, F32), jnp.zeros((tq, 1), F32), jnp.zeros((tq, V_DIM), F32))
    carry = lax.fori_loop(0, qi, lambda kb, c: step(kb, c, False), (one,) * hpb)
    final = step(qi, carry, True)
    o_ref[...] = jnp.concatenate([acc / l_fin for _, l_fin, acc in final], axis=-1).astype(o_ref.dtype)


def mla_prompt_attention(q, kfull, v, *, batch, seq, tq, hpb):
    nq = seq // tq
    assert MLA_HEADS % hpb == 0
    blocks = [_nbytes((tq, hpb * Q_HEAD_PAD), BF16), _nbytes((seq, hpb * Q_HEAD_PAD), BF16),
              _nbytes((seq, hpb * V_DIM), BF16), _nbytes((tq, hpb * V_DIM), BF16)]
    return pl.pallas_call(
        functools.partial(_flash_kernel, tq=tq, scale=MLA_SCALE, hpb=hpb),
        grid=(batch, MLA_HEADS // hpb, nq),
        in_specs=[
            pl.BlockSpec((tq, hpb * Q_HEAD_PAD), lambda b, h, i: (b * nq + i, h)),
            pl.BlockSpec((seq, hpb * Q_HEAD_PAD), lambda b, h, i: (b, h)),
            pl.BlockSpec((seq, hpb * V_DIM), lambda b, h, i: (b, h)),
        ],
        out_specs=pl.BlockSpec((tq, hpb * V_DIM), lambda b, h, i: (b * nq + i, h)),
        out_shape=jax.ShapeDtypeStruct((batch * seq, MLA_WIDTH), BF16),
        compiler_params=pltpu.CompilerParams(
            dimension_semantics=("arbitrary", "arbitrary", "arbitrary"),
            vmem_limit_bytes=_vmem_limit(blocks, 6 * hpb * _nbytes((tq, tq), F32))),
        name="mla_prompt_attention",
    )(q, kfull, v)


def _dec_kernel(pt_ref, q_ref, new_ref, cache_ref, o_ref, buf, sem, *, layer, group, n_groups, t_new, scale):
    b = pl.program_id(0)
    rows_q = q_ref.shape[1]

    def page_copy(bb, g, p):
        page = pt_ref[bb, g * group + p]
        dst = buf.at[g, :, pl.ds(p * PAGE_SIZE, PAGE_SIZE)]
        return pltpu.make_async_copy(cache_ref.at[layer, page], dst, sem.at[g])

    def start(bb, g):
        for p in range(group):
            page_copy(bb, g, p).start(priority=p % 2)

    def wait(bb, g):
        for p in range(group):
            page_copy(bb, g, p).wait()

    @pl.when(b == 0)
    def _():
        for g in range(DEC_AHEAD):
            start(0, g)

    def fold(state, s, vals, vals_dim):
        m_run, l_run, acc = state
        m_new = jnp.maximum(m_run, jnp.max(s, axis=-1, keepdims=True))
        a = jnp.exp(m_run - m_new)
        p = jnp.exp(s - m_new)
        l_new = a * l_run + jnp.sum(p, axis=-1, keepdims=True)
        acc = a * acc + lax.dot_general(p.astype(BF16), vals, (((1,), (vals_dim,)), ((), ())),
                                        preferred_element_type=F32)
        return m_new, l_new, acc

    q = q_ref[0].astype(BF16)
    empty = (jnp.full((rows_q, 1), -jnp.inf, F32), jnp.zeros((rows_q, 1), F32), jnp.zeros((rows_q, KV_LORA), F32))
    states = [empty] * DEC_STREAMS
    span = group * PAGE_SIZE // DEC_STREAMS
    for g in range(n_groups):
        ahead = g + DEC_AHEAD
        if ahead < n_groups:
            start(b, ahead)
        else:
            @pl.when(b + 1 < pl.num_programs(0))
            def _():
                start(b + 1, ahead - n_groups)
        wait(b, g)
        for c in range(DEC_STREAMS):
            keys_t = buf[g, :, c * span:(c + 1) * span].astype(BF16)
            s = jnp.dot(q, keys_t, preferred_element_type=F32) * scale
            states[c] = fold(states[c], s, keys_t[:KV_LORA], 1)
    keys = new_ref[0].astype(BF16)
    s = lax.dot_general(q, keys, (((1,), (1,)), ((), ())), preferred_element_type=F32) * scale
    t_of_row = lax.broadcasted_iota(jnp.int32, (rows_q, t_new), 0) % t_new
    j = lax.broadcasted_iota(jnp.int32, (rows_q, t_new), 1)
    s = jnp.where(j <= t_of_row, s, -jnp.inf)
    states[0] = fold(states[0], s, keys[:, :KV_LORA], 0)
    m_all = functools.reduce(jnp.maximum, [st[0] for st in states])
    l_fin = sum(st[1] * jnp.exp(st[0] - m_all) for st in states)
    acc = sum(st[2] * jnp.exp(st[0] - m_all) for st in states)
    o_ref[0] = acc / l_fin


def mla_sample_attention(qcat, rows_new, cache_t, page_table, *, layer, group):
    nb, rows_q, _ = qcat.shape
    t_new = rows_new.shape[1]
    n_pages = page_table.shape[1]
    n_groups = n_pages // group
    assert n_pages % group == 0 and DEC_AHEAD < n_groups
    grid_spec = pltpu.PrefetchScalarGridSpec(
        num_scalar_prefetch=1,
        grid=(nb,),
        in_specs=[
            pl.BlockSpec((1, rows_q, CACHE_DIM), lambda b, pt: (b, 0, 0)),
            pl.BlockSpec((1, t_new, CACHE_DIM), lambda b, pt: (b, 0, 0)),
            pl.BlockSpec(memory_space=pl.ANY),
        ],
        out_specs=pl.BlockSpec((1, rows_q, KV_LORA), lambda b, pt: (b, 0, 0)),
        scratch_shapes=[
            pltpu.VMEM((n_groups, CACHE_DIM, group * PAGE_SIZE), F32),
            pltpu.SemaphoreType.DMA((n_groups,)),
        ],
    )
    scratch_bytes = (_nbytes((n_groups, CACHE_DIM, group * PAGE_SIZE), F32)
                     + _nbytes((CACHE_DIM, group * PAGE_SIZE), F32)
                     + 6 * _nbytes((rows_q, group * PAGE_SIZE), F32))
    return pl.pallas_call(
        functools.partial(_dec_kernel, layer=layer, group=group, n_groups=n_groups, t_new=t_new, scale=MLA_SCALE),
        grid_spec=grid_spec,
        out_shape=jax.ShapeDtypeStruct((nb, rows_q, KV_LORA), F32),
        compiler_params=pltpu.CompilerParams(
            dimension_semantics=("arbitrary",),
            vmem_limit_bytes=_vmem_limit([_nbytes((rows_q, CACHE_DIM), F32)], scratch_bytes)),
        name="mla_sample_attention",
    )(page_table, qcat, rows_new, cache_t)


def _qlat_kernel(q_ref, w_ref, o_ref, *, nb, t):
    q = q_ref[...]
    lat = jnp.dot(q[:, :QK_NOPE], w_ref[0], preferred_element_type=F32)
    cat = jnp.concatenate([lat, q[:, QK_NOPE:QK_NOPE + QK_ROPE].astype(F32)], axis=-1)
    o_ref[...] = cat.reshape(nb, 1, t, CACHE_DIM).astype(o_ref.dtype)


def sample_query_latent(q, w_lat, *, row0, nb, t):
    m_s = nb * t
    blk0 = row0 // m_s
    return pl.pallas_call(
        functools.partial(_qlat_kernel, nb=nb, t=t),
        grid=(MLA_HEADS,),
        in_specs=[
            pl.BlockSpec((m_s, Q_HEAD_PAD), lambda h: (blk0, h)),
            pl.BlockSpec((1, QK_NOPE, KV_LORA), lambda h: (h, 0, 0)),
        ],
        out_specs=pl.BlockSpec((nb, 1, t, CACHE_DIM), lambda h: (0, h, 0, 0)),
        out_shape=jax.ShapeDtypeStruct((nb, MLA_HEADS, t, CACHE_DIM), F32),
        compiler_params=pltpu.CompilerParams(dimension_semantics=("arbitrary",)),
        name="sample_query_latent",
    )(q, w_lat)


def _oproj_kernel(o_ref, w_ref, a_ref, *, nb, t):
    o = o_ref[...].reshape(nb * t, KV_LORA).astype(BF16)
    a_ref[...] = jnp.dot(o, w_ref[0], preferred_element_type=F32).astype(a_ref.dtype)


def sample_value_proj(o_lat, w_uv, *, nb, t):
    return pl.pallas_call(
        functools.partial(_oproj_kernel, nb=nb, t=t),
        grid=(MLA_HEADS,),
        in_specs=[
            pl.BlockSpec((nb, 1, t, KV_LORA), lambda h: (0, h, 0, 0)),
            pl.BlockSpec((1, KV_LORA, V_DIM), lambda h: (h, 0, 0)),
        ],
        out_specs=pl.BlockSpec((nb * t, V_DIM), lambda h: (0, h)),
        out_shape=jax.ShapeDtypeStruct((nb * t, MLA_WIDTH), BF16),
        compiler_params=pltpu.CompilerParams(dimension_semantics=("arbitrary",)),
        name="sample_value_proj",
    )(o_lat, w_uv)


def _hgrn_kernel(*refs, layer, sub, n_sub, nseq, has_init, wide, hpb):
    if has_init:
        lbl_ref, hq_ref, hf_ref, hi_ref, hg_ref, gain_ref, s0_ref, b_ref, sout_ref, st_ref = refs
    else:
        lbl_ref, hq_ref, hf_ref, hi_ref, hg_ref, gain_ref, b_ref, sout_ref, st_ref = refs
    rows = nseq * n_sub * sub
    logits = lbl_ref[...]
    e = jnp.exp(logits - jnp.max(logits, axis=0, keepdims=True))
    wsm = e / jnp.sum(e, axis=0, keepdims=True)
    lb = jnp.sum(wsm[:layer + 1], axis=0, keepdims=True) - wsm[0:1]

    f = lb + (1.0 - lb) * jax.nn.sigmoid(hf_ref[...])
    logf = jnp.log(f)
    kk = 1.0 - f
    q = jax.nn.silu(hq_ref[...])
    v = hi_ref[...]
    row = lax.broadcasted_iota(jnp.int32, (rows, hpb * HG_DK), 0)

    def block_cumsum(width):
        in_blk = jnp.bitwise_and(row, width - 1)
        acc = logf
        shift = 1
        while shift < width:
            acc = acc + jnp.where(in_blk >= shift, pltpu.roll(acc, shift, 0), 0.0)
            shift *= 2
        return acc

    def lower_tri(width):
        return (lax.broadcasted_iota(jnp.int32, (width, width), 1)
                <= lax.broadcasted_iota(jnp.int32, (width, width), 0))

    def head(x, h):
        return x[:, h * HG_DK:(h + 1) * HG_DK]

    def exact_blocks(sts, lo, hi):
        bcum = block_cumsum(sub)
        qt = q * jnp.exp(bcum)
        tri = lower_tri(sub)
        sts = list(sts)
        outs = [[] for _ in range(hpb)]
        for r0 in range(lo, hi, sub):
            sl = slice(r0, r0 + sub)
            for h in range(hpb):
                bj, qj, kj = head(bcum[sl], h), head(q[sl], h), head(kk[sl], h)
                vj = head(v[sl], h).astype(BF16)
                o_inter = lax.dot_general(head(qt[sl], h).astype(BF16), sts[h].astype(BF16),
                                          (((1,), (1,)), ((), ())), preferred_element_type=F32)
                diff = jnp.minimum(bj[:, None, :] - bj[None, :, :], 0.0)
                att = jnp.sum(qj[:, None, :] * kj[None, :, :] * jnp.exp(diff), axis=-1)
                att = jnp.where(tri, att, 0.0)
                o_intra = jnp.dot(att.astype(BF16), vj, preferred_element_type=F32)
                bl = bj[sub - 1:sub, :]
                kt = (kj * jnp.exp(bl - bj)).astype(BF16)
                upd = lax.dot_general(vj, kt, (((0,), (0,)), ((), ())), preferred_element_type=F32)
                sts[h] = sts[h] * jnp.exp(bl) + upd
                outs[h].append(o_intra + o_inter)
        return tuple(jnp.concatenate(o_h, axis=0) for o_h in outs), tuple(sts)

    def factored_blocks(sts, bcum, lo, hi):
        qt = q * jnp.exp(bcum)
        kt = kk * jnp.exp(-bcum)
        tri = lower_tri(wide)
        sts = list(sts)
        outs = [[] for _ in range(hpb)]
        for r0 in range(lo, hi, wide):
            sl = slice(r0, r0 + wide)
            bl_all = bcum[r0 + wide - 1:r0 + wide, :]
            kl_all = kk[sl] * jnp.exp(bl_all - bcum[sl])
            for h in range(hpb):
                qj, vj = head(qt[sl], h).astype(BF16), head(v[sl], h).astype(BF16)
                att = lax.dot_general(qj, head(kt[sl], h).astype(BF16), (((1,), (1,)), ((), ())),
                                      preferred_element_type=F32)
                att = jnp.where(tri, att, 0.0).astype(BF16)
                o_blk = jnp.dot(att, vj, preferred_element_type=F32)
                o_blk += lax.dot_general(qj, sts[h].astype(BF16), (((1,), (1,)), ((), ())),
                                         preferred_element_type=F32)
                upd = lax.dot_general(vj, head(kl_all, h).astype(BF16), (((0,), (0,)), ((), ())),
                                      preferred_element_type=F32)
                sts[h] = sts[h] * jnp.exp(head(bl_all, h)) + upd
                outs[h].append(o_blk)
        return tuple(jnp.concatenate(o_h, axis=0) for o_h in outs), tuple(sts)

    if not has_init:
        @pl.when(pl.program_id(2) == 0)
        def _():
            st_ref[...] = jnp.zeros_like(st_ref)

    per_seq = n_sub * sub
    outs = []
    for s in range(nseq):
        st0 = tuple(s0_ref[s, h].T if has_init else st_ref[h] for h in range(hpb))
        lo, hi = s * per_seq, (s + 1) * per_seq
        if wide is None:
            o_s, sts = exact_blocks(st0, lo, hi)
        else:
            bwide = block_cumsum(wide)
            safe = jnp.min(bwide) >= -HGRN_SAFE_LOG
            o_s, sts = lax.cond(safe, lambda s_in: factored_blocks(s_in, bwide, lo, hi),
                                lambda s_in: exact_blocks(s_in, lo, hi), st0)
        outs.append(o_s)
        for h in range(hpb):
            if has_init:
                sout_ref[s, h] = sts[h].T
            else:
                st_ref[h] = sts[h]
        if not has_init:
            @pl.when(pl.program_id(2) == pl.num_programs(2) - 1)
            def _():
                for h in range(hpb):
                    sout_ref[0, h] = sts[h].T
    gain = gain_ref[...]
    normed = []
    for h in range(hpb):
        o_h = outs[0][h] if nseq == 1 else jnp.concatenate([o_s[h] for o_s in outs], axis=0)
        normed.append(_rms(o_h, head(gain, h)))
    o = normed[0] if hpb == 1 else jnp.concatenate(normed, axis=-1)
    b_ref[...] = (o * jax.nn.silu(hg_ref[...])).astype(b_ref.dtype)


def hgrn_mixer(hproj, lb_logits, gain, state0, *, layer, row0, nseq_total, seq_len, blk_len, nseq, sub, wide=None,
               hpb=1):
    has_init = state0 is not None
    n_sub = blk_len // sub
    nblk = seq_len // blk_len
    rows = nseq * blk_len
    assert row0 % rows == 0 and nseq_total % nseq == 0 and (nseq == 1 or nblk == 1)
    r0 = row0 // rows
    assert HG_HEADS % hpb == 0
    hg = HG_HEADS // hpb
    wcol = hpb * HG_DK
    col = lambda p: pl.BlockSpec((rows, wcol), lambda s, h, c: (r0 + s * nblk + c, p * hg + h))
    in_specs = [
        pl.BlockSpec((DEPTH, wcol), lambda s, h, c: (0, h)),
        col(0), col(1), col(2), col(3),
        pl.BlockSpec((1, wcol), lambda s, h, c: (0, h)),
    ]
    args = [lb_logits, hproj, hproj, hproj, hproj, gain.reshape(1, HG_WIDTH)]
    if has_init:
        in_specs.append(pl.BlockSpec((None, nseq, hpb, HG_DK, HG_DV), lambda s, h, c: (layer, s, h, 0, 0)))
        args.append(state0)
    out_specs = [
        pl.BlockSpec((rows, wcol), lambda s, h, c: (s * nblk + c, h)),
        pl.BlockSpec((nseq, hpb, HG_DK, HG_DV), lambda s, h, c: (s, h, 0, 0)),
    ]
    out_shape = [
        jax.ShapeDtypeStruct((nseq_total * seq_len, HG_WIDTH), BF16),
        jax.ShapeDtypeStruct((nseq_total, HG_HEADS, HG_DK, HG_DV), F32),
    ]
    return pl.pallas_call(
        functools.partial(_hgrn_kernel, layer=layer, sub=sub, n_sub=n_sub, nseq=nseq, has_init=has_init, wide=wide,
                          hpb=hpb),
        grid=(nseq_total // nseq, hg, nblk),
        in_specs=in_specs,
        out_specs=out_specs,
        out_shape=out_shape,
        scratch_shapes=[pltpu.VMEM((hpb, HG_DV, HG_DK), F32)],
        compiler_params=pltpu.CompilerParams(
            dimension_semantics=("arbitrary", "arbitrary", "arbitrary"),
            vmem_limit_bytes=32 << 20),
        name="hgrn_prompt" if not has_init else "hgrn_sample",
    )(*args)


def _gmlp_kernel(gu_ref, vn_ref, w_ref, bias_ref, o_ref, *, n_chunks):
    for c in range(n_chunks):
        rs = slice(c * GM_CHUNK, (c + 1) * GM_CHUNK)
        vn = vn_ref[rs, :].astype(BF16)
        parts = []
        for g in range(GM_GROUPS):
            cs = slice(g * GM_GROUP_DIM, (g + 1) * GM_GROUP_DIM)
            parts.append(jnp.dot(w_ref[0, g], vn[:, cs], preferred_element_type=F32))
        mixed = jnp.concatenate(parts, axis=-1) + bias_ref[0]
        o_ref[rs, :] = (gu_ref[rs, :].astype(F32) * mixed).astype(o_ref.dtype)


def gmlp_mix(gu_act, vn, wmix, bias, *, tm, prompt_rows):
    m = gu_act.shape[0]
    first_sample_tile = prompt_rows // tm
    kind = lambda i: jnp.where(i >= first_sample_tile, 1, 0)
    return pl.pallas_call(
        functools.partial(_gmlp_kernel, n_chunks=tm // GM_CHUNK),
        grid=(m // tm,),
        in_specs=[
            pl.BlockSpec((tm, GM_WIDTH), lambda i: (i, 0)),
            pl.BlockSpec((tm, GM_WIDTH), lambda i: (i, 0)),
            pl.BlockSpec((1, GM_GROUPS, GM_CHUNK, GM_CHUNK), lambda i: (kind(i), 0, 0, 0)),
            pl.BlockSpec((1, GM_CHUNK, GM_WIDTH), lambda i: (kind(i), 0, 0)),
        ],
        out_specs=pl.BlockSpec((tm, GM_WIDTH), lambda i: (i, 0)),
        out_shape=jax.ShapeDtypeStruct((m, GM_WIDTH), BF16),
        compiler_params=pltpu.CompilerParams(dimension_semantics=("arbitrary",), vmem_limit_bytes=32 << 20),
        name="gmlp_mix",
    )(gu_act, vn, wmix, bias)


def _rope_tile(t, cos_t, sin_t):
    return t * cos_t + pltpu.roll(t, QK_ROPE, 1) * sin_t


def _epi_ffn_up(accs, row_aux, col_aux):
    return [jax.nn.silu(accs[0]) * accs[1]]


def _epi_small(accs, row_aux, col_aux):
    y = accs[0]
    cos_t, sin_t = row_aux
    qg, kvg = col_aux
    cqn = _rms(y[:, :Q_LORA], qg)
    ckvn = _rms(y[:, Q_LORA:Q_LORA + KV_LORA], kvg)
    roped = _rope_tile(y[:, Q_LORA + KV_LORA:], cos_t, sin_t)
    return [cqn, jnp.concatenate([ckvn, roped[:, :QK_ROPE]], axis=-1)]


def _epi_identity(accs, row_aux, col_aux):
    return [accs[0]]


def _epi_gelu(accs, row_aux, col_aux):
    return [jax.nn.gelu(accs[0])]


def _epi_gelu_ln(accs, row_aux, col_aux):
    y = jax.nn.gelu(accs[0])
    g, b = col_aux
    mu = jnp.mean(y, axis=-1, keepdims=True)
    var = jnp.mean(jnp.square(y - mu), axis=-1, keepdims=True)
    return [(y - mu) * lax.rsqrt(var + EPS) * g + b]


def _epi_sigmoid(accs, row_aux, col_aux):
    return [jax.nn.sigmoid(accs[0])]


def _epi_q_rope(accs, row_aux, col_aux):
    y = accs[0]
    cos_t, sin_t = row_aux
    parts = []
    for h in range(MLA_HEADS):
        base = h * Q_HEAD_PAD
        parts.append(y[:, base:base + QK_NOPE])
        parts.append(_rope_tile(y[:, base + QK_NOPE:base + Q_HEAD_PAD], cos_t, sin_t))
    return [jnp.concatenate(parts, axis=-1)]


def _epi_kv_split(accs, row_aux, col_aux):
    n_k = MLA_HEADS * Q_HEAD_PAD
    return [accs[0][:, :n_k], accs[0][:, n_k:]]


def _rot_cols(w):
    half = QK_ROPE // 2
    return jnp.concatenate([-w[..., half:], w[..., :half]], axis=-1)


def _prep_layer(l, w_in, w_uq, w_ukv, gmlp_w_s, gmlp_b_s, t_sample):
    wi = w_in[l]
    o_kr = Q_LORA + KV_LORA
    o_h = o_kr + QK_ROPE
    w_small = jnp.concatenate([wi[:, :o_h], _rot_cols(wi[:, o_kr:o_h])], axis=-1).astype(BF16)
    w_hgrn = wi[:, o_h:o_h + 4 * HG_WIDTH].astype(BF16)
    o_g = o_h + 4 * HG_WIDTH
    w_gu = wi[:, o_g:o_g + GM_WIDTH].astype(BF16)
    w_gv = wi[:, o_g + GM_WIDTH:o_g + 2 * GM_WIDTH].astype(BF16)
    w_gate = wi[:, o_g + 2 * GM_WIDTH:].astype(BF16)

    uq = w_uq[l].reshape(Q_LORA, MLA_HEADS, QK_NOPE + QK_ROPE)
    uq_rope = uq[..., QK_NOPE:]
    w_q = jnp.concatenate([uq[..., :QK_NOPE], uq_rope, _rot_cols(uq_rope)], axis=-1)
    w_q = w_q.reshape(Q_LORA, MLA_HEADS * Q_HEAD_PAD).astype(BF16)

    ukv = w_ukv[l].reshape(KV_LORA, MLA_HEADS, QK_NOPE + V_DIM)
    w_uk, w_uv = ukv[..., :QK_NOPE], ukv[..., QK_NOPE:]
    k_top = jnp.concatenate([w_uk, jnp.zeros((KV_LORA, MLA_HEADS, Q_HEAD_PAD - QK_NOPE), F32)], axis=-1)
    eye = jnp.concatenate([jnp.zeros((QK_ROPE, QK_NOPE), F32), jnp.eye(QK_ROPE, dtype=F32),
                           jnp.zeros((QK_ROPE, Q_HEAD_PAD - QK_NOPE - QK_ROPE), F32)], axis=-1)
    k_bot = jnp.broadcast_to(eye[:, None, :], (QK_ROPE, MLA_HEADS, Q_HEAD_PAD))
    w_k = jnp.concatenate([k_top, k_bot], axis=0).reshape(CACHE_DIM, MLA_HEADS * Q_HEAD_PAD).astype(BF16)
    w_v = jnp.concatenate([w_uv, jnp.zeros((QK_ROPE, MLA_HEADS, V_DIM), F32)], axis=0)
    w_v = w_v.reshape(CACHE_DIM, MLA_WIDTH).astype(BF16)
    w_lat = jnp.transpose(w_uk, (1, 2, 0)).astype(BF16)
    w_uvh = jnp.transpose(w_uv, (1, 0, 2)).astype(BF16)

    ws = gmlp_w_s[l]
    w_prompt = jnp.tril(ws)
    small = jnp.tril(ws[:, :t_sample, :t_sample])
    reps = GM_CHUNK // t_sample
    w_sample = jnp.einsum("ab,gts->gatbs", jnp.eye(reps, dtype=F32), small).reshape(GM_GROUPS, GM_CHUNK, GM_CHUNK)
    wmix = jnp.stack([w_prompt, w_sample]).astype(BF16)
    bs = gmlp_b_s[l]
    bias_p = jnp.repeat(bs.T[:, :, None], GM_GROUP_DIM, axis=2).reshape(GM_CHUNK, GM_WIDTH)
    bias_s = jnp.tile(jnp.repeat(bs[:, :t_sample].T[:, :, None], GM_GROUP_DIM, axis=2).reshape(t_sample, GM_WIDTH),
                      (reps, 1))
    bias = jnp.stack([bias_p, bias_s])
    return dict(w_small=w_small, w_hgrn=w_hgrn, w_gu=w_gu, w_gv=w_gv, w_gate=w_gate, w_q=w_q,
                w_kv=jnp.concatenate([w_k, w_v], axis=-1), w_lat=w_lat, w_uvh=w_uvh, wmix=wmix, bias=bias)


def _rope_tables(pos):
    half = QK_ROPE // 2
    inv = ROPE_THETA ** (-jnp.arange(half, dtype=F32) / half)
    ang = pos.astype(F32)[:, None] * inv[None, :]
    pad = jnp.zeros((pos.shape[0], LANES - QK_ROPE), F32)
    cos_t = jnp.concatenate([jnp.cos(ang), jnp.cos(ang), pad], axis=-1)
    sin_t = jnp.concatenate([jnp.sin(ang), jnp.sin(ang), pad], axis=-1)
    return cos_t, sin_t


def kernel(x_prompt, x_sample, cache_mla, state_hgrn, page_table, norm_gains, w_ffn1_gate, w_ffn1_up, w_ffn1_down, w_ffn2_gate, w_ffn2_up, w_ffn2_down, w_in, mla_q_norm, mla_kv_norm, w_uq, w_ukv, hgrn_lb_logits, hgrn_out_norm, gmlp_ln_g, gmlp_ln_b, gmlp_w_s, gmlp_b_s, w_br_mla, w_br_hgrn, w_br_gmlp, w_out):
    batch, seq, d = x_prompt.shape
    nb, t_s, _ = x_sample.shape
    m_p, m_s = batch * seq, nb * t_s
    m = m_p + m_s
    past = page_table.shape[1] * PAGE_SIZE
    tm = 1024
    while m_p % tm or m_s % tm:
        tm //= 2
    th = min(512, tm)
    tq4 = min(256, tm)
    assert tm >= GM_CHUNK and m_p % m_s == 0

    pos = jnp.concatenate([jnp.tile(jnp.arange(seq), batch), jnp.tile(past + jnp.arange(t_s), nb)])
    cos_t, sin_t = _rope_tables(pos)
    bf = lambda w: w.astype(BF16)
    cache_t = jnp.swapaxes(cache_mla, 2, 3)

    rows_all, s_p_all, s_s_all, vn_all = [], [], [], []
    x, xn = join_rmsnorm(x_prompt.reshape(m_p, d), x_sample.reshape(m_s, d), norm_gains[0, 0], tm=th)
    for l in range(DEPTH):
        ng = norm_gains[l]
        p = _prep_layer(l, w_in, w_uq, w_ukv, gmlp_w_s, gmlp_b_s, t_s)

        hmid = mm(xn, [w_ffn1_gate, w_ffn1_up], _epi_ffn_up, [(512, BF16)], tm=tm, tn=512, layer=l,
                  name="ffn1_up")[0]
        x, hn = mm_res(hmid, bf(w_ffn1_down[l]), x, ng[1], ng[2], alpha=0.5, tm=tq4, name="ffn1_down")

        cqn, rows = mm(hn, [p["w_small"]], _epi_small, [(Q_LORA, BF16), (CACHE_DIM, F32)], tm=th,
                       tn=p["w_small"].shape[1], row_aux=(cos_t, sin_t),
                       col_aux=(mla_q_norm[l].reshape(1, -1), mla_kv_norm[l].reshape(1, -1)), name="in_proj_mla")
        hproj = mm(hn, [p["w_hgrn"]], _epi_identity, [(1024, F32)], tm=tm, tn=1024, name="in_proj_hgrn")[0]
        gu_act = mm(hn, [p["w_gu"]], _epi_gelu, [(GM_WIDTH, BF16)], tm=tm, tn=GM_WIDTH, name="in_proj_gu")[0]
        vn = mm(hn, [p["w_gv"]], _epi_gelu_ln, [(GM_WIDTH, F32)], tm=tm, tn=GM_WIDTH,
                col_aux=(gmlp_ln_g[l].reshape(1, -1), gmlp_ln_b[l].reshape(1, -1)), name="in_proj_gv")[0]
        gate = mm(hn, [p["w_gate"]], _epi_sigmoid, [(1024, BF16)], tm=tm, tn=1024, name="in_proj_gate")[0]

        q = mm(cqn, [p["w_q"]], _epi_q_rope, [(MLA_HEADS * Q_HEAD_PAD, BF16)], tm=th, tn=MLA_HEADS * Q_HEAD_PAD,
               row_aux=(cos_t, sin_t), name="mla_q")[0]
        kfull, vv = mm(rows, [p["w_kv"]], _epi_kv_split, [(MLA_HEADS * Q_HEAD_PAD, BF16), (MLA_WIDTH, BF16)],
                       tm=th, tn=p["w_kv"].shape[1], rows=m_p, name="mla_kv")
        a_p = mla_prompt_attention(q, kfull, vv, batch=batch, seq=seq, tq=min(1024, seq), hpb=2)

        qcat = sample_query_latent(q, p["w_lat"], row0=m_p, nb=nb, t=t_s)
        o_lat = mla_sample_attention(qcat.reshape(nb, MLA_HEADS * t_s, CACHE_DIM), rows[m_p:].reshape(nb, t_s, CACHE_DIM),
                                     cache_t, page_table, layer=l, group=32)
        a_s = sample_value_proj(o_lat.reshape(nb, MLA_HEADS, t_s, KV_LORA), p["w_uvh"], nb=nb, t=t_s)
        a = jnp.concatenate([a_p, a_s], axis=0)

        b_p, s_p = hgrn_mixer(hproj, hgrn_lb_logits, hgrn_out_norm[l], None, layer=l, row0=0, nseq_total=batch,
                              seq_len=seq, blk_len=256, nseq=1, sub=HGRN_SUB, wide=HGRN_WIDE, hpb=4)
        b_s, s_s = hgrn_mixer(hproj, hgrn_lb_logits, hgrn_out_norm[l], state_hgrn, layer=l, row0=m_p,
                              nseq_total=nb, seq_len=t_s, blk_len=t_s, nseq=8, sub=t_s, hpb=4)
        bmix = jnp.concatenate([b_p, b_s], axis=0)

        cmix = gmlp_mix(gu_act, vn, p["wmix"], p["bias"], tm=th, prompt_rows=m_p)

        merged = merge_branches(a, bmix, cmix, gate, bf(w_br_mla[l]), bf(w_br_hgrn[l]), bf(w_br_gmlp[l]), tm=th)
        x, hn4 = mm_res(merged, bf(w_out[l]), x, ng[3], ng[4], alpha=1.0, tm=th, name="merge_out")

        hmid = mm(hn4, [w_ffn2_gate, w_ffn2_up], _epi_ffn_up, [(512, BF16)], tm=tm, tn=512, layer=l,
                  name="ffn2_up")[0]
        g_next = norm_gains[l + 1, 0] if l + 1 < DEPTH else None
        x, xn = mm_res(hmid, bf(w_ffn2_down[l]), x, ng[5], g_next, alpha=0.5, tm=tq4, name="ffn2_down")

        rows_all.append(rows)
        s_p_all.append(s_p)
        s_s_all.append(s_s)
        vn_all.append(vn[m_p:])

    rows_st = jnp.stack(rows_all)
    return (x[:m_p].reshape(batch, seq, d), x[m_p:].reshape(nb, t_s, d),
            rows_st[:, :m_p].reshape(DEPTH, batch, seq, CACHE_DIM), rows_st[:, m_p:].reshape(DEPTH, nb, t_s, CACHE_DIM),
            jnp.stack(s_p_all), jnp.stack(s_s_all), jnp.stack(vn_all).reshape(DEPTH, nb, t_s, GM_WIDTH))
```

```python
import functools

import jax
import jax.numpy as jnp
import numpy as np
from jax import lax
from jax.experimental import pallas as pl
from jax.experimental.pallas import tpu as pltpu

D_MODEL = 2048
DEPTH = 2
PAGE_SIZE = 128
MLA_HEADS = 8
QK_NOPE = 128
QK_ROPE = 64
V_DIM = 128
Q_LORA = 512
KV_LORA = 256
ROPE_THETA = 10000.0
MLA_SCALE = (QK_NOPE + QK_ROPE) ** -0.5
MLA_WIDTH = MLA_HEADS * V_DIM
CACHE_DIM = KV_LORA + QK_ROPE
HG_HEADS = 8
HG_DK = 128
HG_DV = 128
HG_WIDTH = HG_HEADS * HG_DV
GM_GROUPS = 8
GM_GROUP_DIM = 128
GM_CHUNK = 128
GM_WIDTH = GM_GROUPS * GM_GROUP_DIM
D_FF = 5632
N_BRANCH = 3
EPS = 1e-6

LANES = 128
Q_HEAD_PAD = 2 * LANES
VMEM_CAP_BYTES = 60000 * 1024
DEC_STREAMS = 4
DEC_AHEAD = 3
HGRN_SUB = 16
HGRN_WIDE = 64
HGRN_SAFE_LOG = 80.0

F32 = jnp.float32
BF16 = jnp.bfloat16


def _vmem_limit(block_bytes, temp_bytes=0):
    est = 2 * sum(block_bytes) + temp_bytes + (4 << 20)
    return int(min(max(est, 16 << 20), VMEM_CAP_BYTES))


def _nbytes(shape, dtype):
    return int(np.prod(shape)) * jnp.dtype(dtype).itemsize


def _rms(y, g):
    return y * lax.rsqrt(jnp.mean(y * y, axis=-1, keepdims=True) + EPS) * g


def _rmsnorm_kernel(xa_ref, xb_ref, g_ref, x_ref, o_ref, *, tiles_a):
    @pl.when(pl.program_id(0) < tiles_a)
    def _():
        x_ref[...] = xa_ref[...]

    @pl.when(pl.program_id(0) >= tiles_a)
    def _():
        x_ref[...] = xb_ref[...]

    o_ref[...] = _rms(x_ref[...], g_ref[...]).astype(o_ref.dtype)


def join_rmsnorm(xa, xb, g, *, tm):
    (ma, d), mb = xa.shape, xb.shape[0]
    assert ma % tm == 0 and mb % tm == 0
    ta = ma // tm
    m = ma + mb
    tile = lambda: pl.BlockSpec((tm, d), lambda i: (i, 0))
    return pl.pallas_call(
        functools.partial(_rmsnorm_kernel, tiles_a=ta),
        grid=(m // tm,),
        in_specs=[pl.BlockSpec((tm, d), lambda i: (jnp.minimum(i, ta - 1), 0)),
                  pl.BlockSpec((tm, d), lambda i: (jnp.maximum(i - ta, 0), 0)),
                  pl.BlockSpec((1, d), lambda i: (0, 0))],
        out_specs=[tile(), tile()],
        out_shape=[jax.ShapeDtypeStruct((m, d), F32), jax.ShapeDtypeStruct((m, d), BF16)],
        compiler_params=pltpu.CompilerParams(
            dimension_semantics=("arbitrary",),
            vmem_limit_bytes=_vmem_limit([3 * _nbytes((tm, d), F32), _nbytes((tm, d), BF16)], _nbytes((tm, d), F32))),
        name="join_rmsnorm",
    )(xa, xb, g.reshape(1, d))


def _mm_kernel(*refs, n_w, n_row, n_col, n_out, epi, cast_w):
    x_ref = refs[0]
    w_refs = refs[1:1 + n_w]
    row_refs = refs[1 + n_w:1 + n_w + n_row]
    col_refs = refs[1 + n_w + n_row:1 + n_w + n_row + n_col]
    out_refs = refs[1 + n_w + n_row + n_col:1 + n_w + n_row + n_col + n_out]
    if cast_w:
        wb_refs = refs[len(refs) - n_w:]

        @pl.when(pl.program_id(1) == 0)
        def _():
            for w, wb in zip(w_refs, wb_refs):
                wb[...] = w[...].astype(BF16)

        w_refs = wb_refs
    x = x_ref[...].astype(BF16)
    accs = [jnp.dot(x, w[...], preferred_element_type=F32) for w in w_refs]
    outs = epi(accs, [r[...] for r in row_refs], [c[...] for c in col_refs])
    for o_ref, o in zip(out_refs, outs):
        o_ref[...] = o.astype(o_ref.dtype)


def mm(x, ws, epi, outs, *, tm, tn, rows=None, row_aux=(), col_aux=(), layer=None, name):
    m = rows if rows is not None else x.shape[0]
    k = x.shape[1]
    n = ws[0].shape[-1]
    nt = n // tn
    cast_w = layer is not None
    assert m % tm == 0 and n % tn == 0
    in_specs = [pl.BlockSpec((tm, k), lambda j, i: (i, 0))]
    if cast_w:
        in_specs += [pl.BlockSpec((None, k, tn), lambda j, i: (layer, 0, j)) for _ in ws]
    else:
        in_specs += [pl.BlockSpec((k, tn), lambda j, i: (0, j)) for _ in ws]
    in_specs += [pl.BlockSpec((tm, a.shape[1]), lambda j, i: (i, 0)) for a in row_aux]
    in_specs += [pl.BlockSpec((1, a.shape[1] // nt), lambda j, i: (0, j)) for a in col_aux]
    out_specs = [pl.BlockSpec((tm, w), lambda j, i: (i, j)) for w, _ in outs]
    out_shape = [jax.ShapeDtypeStruct((m, w * nt), dt) for w, dt in outs]
    blocks = [_nbytes((tm, k), x.dtype)] + [_nbytes((k, tn), ws[0].dtype)] * len(ws)
    blocks += [_nbytes((tm, a.shape[1]), a.dtype) for a in row_aux]
    blocks += [_nbytes((tm, w), dt) for w, dt in outs]
    scratch = [pltpu.VMEM((k, tn), BF16) for _ in ws] if cast_w else []
    res = pl.pallas_call(
        functools.partial(_mm_kernel, n_w=len(ws), n_row=len(row_aux), n_col=len(col_aux), n_out=len(outs), epi=epi,
                          cast_w=cast_w),
        grid=(nt, m // tm),
        in_specs=in_specs,
        out_specs=out_specs,
        out_shape=out_shape,
        scratch_shapes=scratch,
        compiler_params=pltpu.CompilerParams(
            dimension_semantics=("arbitrary", "arbitrary"),
            vmem_limit_bytes=_vmem_limit(blocks, (3 * _nbytes((tm, tn), F32) + cast_w * _nbytes((k, tn), BF16))
                                         * len(ws))),
        name=name,
    )(x, *ws, *row_aux, *col_aux)
    return res


def _mm_res_kernel(*refs, alpha, has_next):
    if has_next:
        h_ref, w_ref, x_ref, gp_ref, gn_ref, xo_ref, no_ref = refs
    else:
        h_ref, w_ref, x_ref, gp_ref, xo_ref = refs
    y = jnp.dot(h_ref[...], w_ref[...], preferred_element_type=F32)
    xo = x_ref[...] + alpha * _rms(y, gp_ref[...])
    xo_ref[...] = xo
    if has_next:
        no_ref[...] = _rms(xo, gn_ref[...]).astype(no_ref.dtype)


def mm_res(h, w, x, g_post, g_next, *, alpha, tm, name):
    m, kdim = h.shape
    d = w.shape[1]
    assert m % tm == 0
    has_next = g_next is not None
    once = pl.Buffered(1)
    in_specs = [
        pl.BlockSpec((tm, kdim), lambda i: (i, 0)),
        pl.BlockSpec((kdim, d), lambda i: (0, 0), pipeline_mode=once),
        pl.BlockSpec((tm, d), lambda i: (i, 0)),
        pl.BlockSpec((1, d), lambda i: (0, 0)),
    ]
    args = [h, w, x, g_post.reshape(1, d)]
    out_specs = [pl.BlockSpec((tm, d), lambda i: (i, 0))]
    out_shape = [jax.ShapeDtypeStruct((m, d), F32)]
    streamed = [_nbytes((tm, kdim), BF16), 2 * _nbytes((tm, d), F32)]
    if has_next:
        in_specs.append(pl.BlockSpec((1, d), lambda i: (0, 0)))
        args.append(g_next.reshape(1, d))
        out_specs.append(pl.BlockSpec((tm, d), lambda i: (i, 0)))
        out_shape.append(jax.ShapeDtypeStruct((m, d), BF16))
        streamed.append(_nbytes((tm, d), BF16))
    res = pl.pallas_call(
        functools.partial(_mm_res_kernel, alpha=alpha, has_next=has_next),
        grid=(m // tm,),
        in_specs=in_specs,
        out_specs=out_specs,
        out_shape=out_shape,
        compiler_params=pltpu.CompilerParams(
            dimension_semantics=("arbitrary",),
            vmem_limit_bytes=_vmem_limit(streamed, _nbytes((kdim, d), BF16) + 3 * _nbytes((tm, d), F32))),
        name=name,
    )(*args)
    return (res[0], res[1]) if has_next else (res[0], None)


def _merge_kernel(ap_ref, as_ref, bp_ref, bs_ref, c_ref, g_ref, wa_ref, wb_ref, wc_ref, o_ref, a_ref, b_ref, *,
                  d, tiles_p):
    @pl.when(pl.program_id(0) < tiles_p)
    def _():
        a_ref[...] = ap_ref[...]
        b_ref[...] = bp_ref[...]

    @pl.when(pl.program_id(0) >= tiles_p)
    def _():
        a_ref[...] = as_ref[...]
        b_ref[...] = bs_ref[...]

    g = g_ref[...].astype(F32)
    acc = g[:, :d] * jnp.dot(a_ref[...], wa_ref[...], preferred_element_type=F32)
    acc += g[:, d:2 * d] * jnp.dot(b_ref[...], wb_ref[...], preferred_element_type=F32)
    acc += g[:, 2 * d:] * jnp.dot(c_ref[...], wc_ref[...], preferred_element_type=F32)
    o_ref[...] = acc.astype(o_ref.dtype)


def merge_branches(a_p, a_s, b_p, b_s, c, g, wa, wb, wc, *, tm):
    (m_p, kw), m_s = a_p.shape, a_s.shape[0]
    assert m_p % tm == 0 and m_s % tm == 0
    tp = m_p // tm
    m = m_p + m_s
    d = wa.shape[1]
    row = lambda w: pl.BlockSpec((tm, w), lambda i: (i, 0))
    seg_p = pl.BlockSpec((tm, kw), lambda i: (jnp.minimum(i, tp - 1), 0))
    seg_s = pl.BlockSpec((tm, kw), lambda i: (jnp.maximum(i - tp, 0), 0))
    full = pl.BlockSpec((kw, d), lambda i: (0, 0))
    blocks = [5 * _nbytes((tm, kw), BF16), _nbytes((tm, 3 * d), BF16), 3 * _nbytes((kw, d), BF16), _nbytes((tm, d), BF16)]
    return pl.pallas_call(
        functools.partial(_merge_kernel, d=d, tiles_p=tp),
        grid=(m // tm,),
        in_specs=[seg_p, seg_s, seg_p, seg_s, row(kw), row(3 * d), full, full, full],
        out_specs=row(d),
        out_shape=jax.ShapeDtypeStruct((m, d), BF16),
        scratch_shapes=[pltpu.VMEM((tm, kw), BF16), pltpu.VMEM((tm, kw), BF16)],
        compiler_params=pltpu.CompilerParams(
            dimension_semantics=("arbitrary",),
            vmem_limit_bytes=_vmem_limit(blocks, 4 * _nbytes((tm, d), F32) + 2 * _nbytes((tm, kw), BF16))),
        name="merge_branches",
    )(a_p, a_s, b_p, b_s, c, g, wa, wb, wc)


def _flash_kernel(q_ref, k_ref, v_ref, o_ref, *, tq, scale, hpb):
    qi = pl.program_id(2)
    c2 = scale * np.log2(np.e)
    qs = [q_ref[:, h * Q_HEAD_PAD:(h + 1) * Q_HEAD_PAD] for h in range(hpb)]

    def step(kb, carry, masked):
        start = pl.multiple_of(kb * tq, tq)
        new = []
        for h in range(hpb):
            m_prev, l_prev, acc = carry[h]
            ks = k_ref[pl.ds(start, tq), h * Q_HEAD_PAD:(h + 1) * Q_HEAD_PAD]
            vs = v_ref[pl.ds(start, tq), h * V_DIM:(h + 1) * V_DIM]
            s = lax.dot_general(qs[h], ks, (((1,), (1,)), ((), ())), preferred_element_type=F32)
            if masked:
                r = lax.broadcasted_iota(jnp.int32, (tq, tq), 0)
                c = lax.broadcasted_iota(jnp.int32, (tq, tq), 1)
                s = jnp.where(c <= r, s, -jnp.inf)
            m_new = jnp.maximum(m_prev, jnp.max(s, axis=-1, keepdims=True))
            a = jnp.exp2((m_prev - m_new) * c2)
            p = jnp.exp2((s - m_new) * c2)
            l_new = a * l_prev + jnp.sum(p, axis=-1, keepdims=True)
            acc = a * acc + jnp.dot(p.astype(BF16), vs, preferred_element_type=F32)
            new.append((m_new, l_new, acc))
        return tuple(new)

    one = (jnp.full((tq, 1), -jnp.inf, F32), jnp.zeros((tq, 1), F32), jnp.zeros((tq, V_DIM), F32))
    carry = lax.fori_loop(0, qi, lambda kb, c: step(kb, c, False), (one,) * hpb)
    final = step(qi, carry, True)
    o_ref[...] = jnp.concatenate([acc / l_fin for _, l_fin, acc in final], axis=-1).astype(o_ref.dtype)


def mla_prompt_attention(q, kfull, v, *, batch, seq, tq, hpb):
    nq = seq // tq
    assert MLA_HEADS % hpb == 0
    blocks = [_nbytes((tq, hpb * Q_HEAD_PAD), BF16), _nbytes((seq, hpb * Q_HEAD_PAD), BF16),
              _nbytes((seq, hpb * V_DIM), BF16), _nbytes((tq, hpb * V_DIM), BF16)]
    return pl.pallas_call(
        functools.partial(_flash_kernel, tq=tq, scale=MLA_SCALE, hpb=hpb),
        grid=(batch, MLA_HEADS // hpb, nq),
        in_specs=[
            pl.BlockSpec((tq, hpb * Q_HEAD_PAD), lambda b, h, i: (b * nq + i, h)),
            pl.BlockSpec((seq, hpb * Q_HEAD_PAD), lambda b, h, i: (b, h)),
            pl.BlockSpec((seq, hpb * V_DIM), lambda b, h, i: (b, h)),
        ],
        out_specs=pl.BlockSpec((tq, hpb * V_DIM), lambda b, h, i: (b * nq + i, h)),
        out_shape=jax.ShapeDtypeStruct((batch * seq, MLA_WIDTH), BF16),
        compiler_params=pltpu.CompilerParams(
            dimension_semantics=("arbitrary", "arbitrary", "arbitrary"),
            vmem_limit_bytes=_vmem_limit(blocks, 6 * hpb * _nbytes((tq, tq), F32))),
        name="mla_prompt_attention",
    )(q, kfull, v)


def _dec_kernel(pt_ref, q_ref, new_ref, cache_ref, o_ref, buf, sem, *, layer, group, n_groups, t_new, scale):
    b = pl.program_id(0)
    rows_q = q_ref.shape[1]

    def page_copy(bb, g, p):
        page = pt_ref[bb, g * group + p]
        dst = buf.at[g, :, pl.ds(p * PAGE_SIZE, PAGE_SIZE)]
        return pltpu.make_async_copy(cache_ref.at[layer, page], dst, sem.at[g])

    def start(bb, g):
        for p in range(group):
            page_copy(bb, g, p).start(priority=p % 2)

    def wait(bb, g):
        for p in range(group):
            page_copy(bb, g, p).wait()

    @pl.when(b == 0)
    def _():
        for g in range(DEC_AHEAD):
            start(0, g)

    def fold(state, s, vals, vals_dim):
        m_run, l_run, acc = state
        m_new = jnp.maximum(m_run, jnp.max(s, axis=-1, keepdims=True))
        a = jnp.exp(m_run - m_new)
        p = jnp.exp(s - m_new)
        l_new = a * l_run + jnp.sum(p, axis=-1, keepdims=True)
        acc = a * acc + lax.dot_general(p.astype(BF16), vals, (((1,), (vals_dim,)), ((), ())),
                                        preferred_element_type=F32)
        return m_new, l_new, acc

    q = q_ref[0].astype(BF16)
    empty = (jnp.full((rows_q, 1), -jnp.inf, F32), jnp.zeros((rows_q, 1), F32), jnp.zeros((rows_q, KV_LORA), F32))
    states = [empty] * DEC_STREAMS
    span = group * PAGE_SIZE // DEC_STREAMS
    for g in range(n_groups):
        ahead = g + DEC_AHEAD
        if ahead < n_groups:
            start(b, ahead)
        else:
            @pl.when(b + 1 < pl.num_programs(0))
            def _():
                start(b + 1, ahead - n_groups)
        wait(b, g)
        for c in range(DEC_STREAMS):
            keys_t = buf[g, :, c * span:(c + 1) * span].astype(BF16)
            s = jnp.dot(q, keys_t, preferred_element_type=F32) * scale
            states[c] = fold(states[c], s, keys_t[:KV_LORA], 1)
    keys = new_ref[0].astype(BF16)
    s = lax.dot_general(q, keys, (((1,), (1,)), ((), ())), preferred_element_type=F32) * scale
    t_of_row = lax.broadcasted_iota(jnp.int32, (rows_q, t_new), 0) % t_new
    j = lax.broadcasted_iota(jnp.int32, (rows_q, t_new), 1)
    s = jnp.where(j <= t_of_row, s, -jnp.inf)
    states[0] = fold(states[0], s, keys[:, :KV_LORA], 0)
    m_all = functools.reduce(jnp.maximum, [st[0] for st in states])
    l_fin = sum(st[1] * jnp.exp(st[0] - m_all) for st in states)
    acc = sum(st[2] * jnp.exp(st[0] - m_all) for st in states)
    o_ref[0] = acc / l_fin


def mla_sample_attention(qcat, rows_new, cache_t, page_table, *, layer, group):
    nb, rows_q, _ = qcat.shape
    t_new = rows_new.shape[1]
    n_pages = page_table.shape[1]
    n_groups = n_pages // group
    assert n_pages % group == 0 and DEC_AHEAD < n_groups
    grid_spec = pltpu.PrefetchScalarGridSpec(
        num_scalar_prefetch=1,
        grid=(nb,),
        in_specs=[
            pl.BlockSpec((1, rows_q, CACHE_DIM), lambda b, pt: (b, 0, 0)),
            pl.BlockSpec((1, t_new, CACHE_DIM), lambda b, pt: (b, 0, 0)),
            pl.BlockSpec(memory_space=pl.ANY),
        ],
        out_specs=pl.BlockSpec((1, rows_q, KV_LORA), lambda b, pt: (b, 0, 0)),
        scratch_shapes=[
            pltpu.VMEM((n_groups, CACHE_DIM, group * PAGE_SIZE), F32),
            pltpu.SemaphoreType.DMA((n_groups,)),
        ],
    )
    scratch_bytes = (_nbytes((n_groups, CACHE_DIM, group * PAGE_SIZE), F32)
                     + _nbytes((CACHE_DIM, group * PAGE_SIZE), F32)
                     + 6 * _nbytes((rows_q, group * PAGE_SIZE), F32))
    return pl.pallas_call(
        functools.partial(_dec_kernel, layer=layer, group=group, n_groups=n_groups, t_new=t_new, scale=MLA_SCALE),
        grid_spec=grid_spec,
        out_shape=jax.ShapeDtypeStruct((nb, rows_q, KV_LORA), F32),
        compiler_params=pltpu.CompilerParams(
            dimension_semantics=("arbitrary",),
            vmem_limit_bytes=_vmem_limit([_nbytes((rows_q, CACHE_DIM), F32)], scratch_bytes)),
        name="mla_sample_attention",
    )(page_table, qcat, rows_new, cache_t)


def _qlat_kernel(q_ref, w_ref, o_ref, *, nb, t):
    q = q_ref[...]
    lat = jnp.dot(q[:, :QK_NOPE], w_ref[0], preferred_element_type=F32)
    cat = jnp.concatenate([lat, q[:, QK_NOPE:QK_NOPE + QK_ROPE].astype(F32)], axis=-1)
    o_ref[...] = cat.reshape(nb, 1, t, CACHE_DIM).astype(o_ref.dtype)


def sample_query_latent(q, w_lat, *, row0, nb, t):
    m_s = nb * t
    blk0 = row0 // m_s
    return pl.pallas_call(
        functools.partial(_qlat_kernel, nb=nb, t=t),
        grid=(MLA_HEADS,),
        in_specs=[
            pl.BlockSpec((m_s, Q_HEAD_PAD), lambda h: (blk0, h)),
            pl.BlockSpec((1, QK_NOPE, KV_LORA), lambda h: (h, 0, 0)),
        ],
        out_specs=pl.BlockSpec((nb, 1, t, CACHE_DIM), lambda h: (0, h, 0, 0)),
        out_shape=jax.ShapeDtypeStruct((nb, MLA_HEADS, t, CACHE_DIM), F32),
        compiler_params=pltpu.CompilerParams(dimension_semantics=("arbitrary",)),
        name="sample_query_latent",
    )(q, w_lat)


def _oproj_kernel(o_ref, w_ref, a_ref, *, nb, t):
    o = o_ref[...].reshape(nb * t, KV_LORA).astype(BF16)
    a_ref[...] = jnp.dot(o, w_ref[0], preferred_element_type=F32).astype(a_ref.dtype)


def sample_value_proj(o_lat, w_uv, *, nb, t):
    return pl.pallas_call(
        functools.partial(_oproj_kernel, nb=nb, t=t),
        grid=(MLA_HEADS,),
        in_specs=[
            pl.BlockSpec((nb, 1, t, KV_LORA), lambda h: (0, h, 0, 0)),
            pl.BlockSpec((1, KV_LORA, V_DIM), lambda h: (h, 0, 0)),
        ],
        out_specs=pl.BlockSpec((nb * t, V_DIM), lambda h: (0, h)),
        out_shape=jax.ShapeDtypeStruct((nb * t, MLA_WIDTH), BF16),
        compiler_params=pltpu.CompilerParams(dimension_semantics=("arbitrary",)),
        name="sample_value_proj",
    )(o_lat, w_uv)


def _hgrn_kernel(*refs, layer, sub, n_sub, nseq, has_init, wide, hpb):
    if has_init:
        lbl_ref, hq_ref, hf_ref, hi_ref, hg_ref, gain_ref, s0_ref, b_ref, sout_ref, st_ref = refs
    else:
        lbl_ref, hq_ref, hf_ref, hi_ref, hg_ref, gain_ref, b_ref, sout_ref, st_ref = refs
    rows = nseq * n_sub * sub
    logits = lbl_ref[...]
    e = jnp.exp(logits - jnp.max(logits, axis=0, keepdims=True))
    wsm = e / jnp.sum(e, axis=0, keepdims=True)
    lb = jnp.sum(wsm[:layer + 1], axis=0, keepdims=True) - wsm[0:1]

    f = lb + (1.0 - lb) * jax.nn.sigmoid(hf_ref[...])
    logf = jnp.log(f)
    kk = 1.0 - f
    q = jax.nn.silu(hq_ref[...])
    v = hi_ref[...]
    row = lax.broadcasted_iota(jnp.int32, (rows, hpb * HG_DK), 0)

    def block_cumsum(width):
        in_blk = jnp.bitwise_and(row, width - 1)
        acc = logf
        shift = 1
        while shift < width:
            acc = acc + jnp.where(in_blk >= shift, pltpu.roll(acc, shift, 0), 0.0)
            shift *= 2
        return acc

    def lower_tri(width):
        return (lax.broadcasted_iota(jnp.int32, (width, width), 1)
                <= lax.broadcasted_iota(jnp.int32, (width, width), 0))

    def head(x, h):
        return x[:, h * HG_DK:(h + 1) * HG_DK]

    def exact_blocks(sts, lo, hi):
        bcum = block_cumsum(sub)
        qt = q * jnp.exp(bcum)
        tri = lower_tri(sub)
        sts = list(sts)
        outs = [[] for _ in range(hpb)]
        for r0 in range(lo, hi, sub):
            sl = slice(r0, r0 + sub)
            for h in range(hpb):
                bj, qj, kj = head(bcum[sl], h), head(q[sl], h), head(kk[sl], h)
                vj = head(v[sl], h).astype(BF16)
                o_inter = lax.dot_general(head(qt[sl], h).astype(BF16), sts[h].astype(BF16),
                                          (((1,), (1,)), ((), ())), preferred_element_type=F32)
                diff = jnp.minimum(bj[:, None, :] - bj[None, :, :], 0.0)
                att = jnp.sum(qj[:, None, :] * kj[None, :, :] * jnp.exp(diff), axis=-1)
                att = jnp.where(tri, att, 0.0)
                o_intra = jnp.dot(att.astype(BF16), vj, preferred_element_type=F32)
                bl = bj[sub - 1:sub, :]
                kt = (kj * jnp.exp(bl - bj)).astype(BF16)
                upd = lax.dot_general(vj, kt, (((0,), (0,)), ((), ())), preferred_element_type=F32)
                sts[h] = sts[h] * jnp.exp(bl) + upd
                outs[h].append(o_intra + o_inter)
        return tuple(jnp.concatenate(o_h, axis=0) for o_h in outs), tuple(sts)

    def factored_blocks(sts, bcum, lo, hi):
        qt = q * jnp.exp(bcum)
        kt = kk * jnp.exp(-bcum)
        tri = lower_tri(wide)
        sts = list(sts)
        outs = [[] for _ in range(hpb)]
        for r0 in range(lo, hi, wide):
            sl = slice(r0, r0 + wide)
            bl_all = bcum[r0 + wide - 1:r0 + wide, :]
            kl_all = kk[sl] * jnp.exp(bl_all - bcum[sl])
            for h in range(hpb):
                qj, vj = head(qt[sl], h).astype(BF16), head(v[sl], h).astype(BF16)
                att = lax.dot_general(qj, head(kt[sl], h).astype(BF16), (((1,), (1,)), ((), ())),
                                      preferred_element_type=F32)
                att = jnp.where(tri, att, 0.0).astype(BF16)
                o_blk = jnp.dot(att, vj, preferred_element_type=F32)
                o_blk += lax.dot_general(qj, sts[h].astype(BF16), (((1,), (1,)), ((), ())),
                                         preferred_element_type=F32)
                upd = lax.dot_general(vj, head(kl_all, h).astype(BF16), (((0,), (0,)), ((), ())),
                                      preferred_element_type=F32)
                sts[h] = sts[h] * jnp.exp(head(bl_all, h)) + upd
                outs[h].append(o_blk)
        return tuple(jnp.concatenate(o_h, axis=0) for o_h in outs), tuple(sts)

    if not has_init:
        @pl.when(pl.program_id(2) == 0)
        def _():
            st_ref[...] = jnp.zeros_like(st_ref)

    per_seq = n_sub * sub
    outs = []
    for s in range(nseq):
        st0 = tuple(s0_ref[s, h].T if has_init else st_ref[h] for h in range(hpb))
        lo, hi = s * per_seq, (s + 1) * per_seq
        if wide is None:
            o_s, sts = exact_blocks(st0, lo, hi)
        else:
            bwide = block_cumsum(wide)
            safe = jnp.min(bwide) >= -HGRN_SAFE_LOG
            o_s, sts = lax.cond(safe, lambda s_in: factored_blocks(s_in, bwide, lo, hi),
                                lambda s_in: exact_blocks(s_in, lo, hi), st0)
        outs.append(o_s)
        for h in range(hpb):
            if has_init:
                sout_ref[s, h] = sts[h].T
            else:
                st_ref[h] = sts[h]
        if not has_init:
            @pl.when(pl.program_id(2) == pl.num_programs(2) - 1)
            def _():
                for h in range(hpb):
                    sout_ref[0, h] = sts[h].T
    gain = gain_ref[...]
    normed = []
    for h in range(hpb):
        o_h = outs[0][h] if nseq == 1 else jnp.concatenate([o_s[h] for o_s in outs], axis=0)
        normed.append(_rms(o_h, head(gain, h)))
    o = normed[0] if hpb == 1 else jnp.concatenate(normed, axis=-1)
    b_ref[...] = (o * jax.nn.silu(hg_ref[...])).astype(b_ref.dtype)


def hgrn_mixer(hproj, lb_logits, gain, state0, *, layer, row0, nseq_total, seq_len, blk_len, nseq, sub, wide=None,
               hpb=1):
    has_init = state0 is not None
    n_sub = blk_len // sub
    nblk = seq_len // blk_len
    rows = nseq * blk_len
    assert row0 % rows == 0 and nseq_total % nseq == 0 and (nseq == 1 or nblk == 1)
    r0 = row0 // rows
    assert HG_HEADS % hpb == 0
    hg = HG_HEADS // hpb
    wcol = hpb * HG_DK
    col = lambda p: pl.BlockSpec((rows, wcol), lambda s, h, c: (r0 + s * nblk + c, p * hg + h))
    in_specs = [
        pl.BlockSpec((DEPTH, wcol), lambda s, h, c: (0, h)),
        col(0), col(1), col(2), col(3),
        pl.BlockSpec((1, wcol), lambda s, h, c: (0, h)),
    ]
    args = [lb_logits, hproj, hproj, hproj, hproj, gain.reshape(1, HG_WIDTH)]
    if has_init:
        in_specs.append(pl.BlockSpec((None, nseq, hpb, HG_DK, HG_DV), lambda s, h, c: (layer, s, h, 0, 0)))
        args.append(state0)
    out_specs = [
        pl.BlockSpec((rows, wcol), lambda s, h, c: (s * nblk + c, h)),
        pl.BlockSpec((nseq, hpb, HG_DK, HG_DV), lambda s, h, c: (s, h, 0, 0)),
    ]
    out_shape = [
        jax.ShapeDtypeStruct((nseq_total * seq_len, HG_WIDTH), BF16),
        jax.ShapeDtypeStruct((nseq_total, HG_HEADS, HG_DK, HG_DV), F32),
    ]
    return pl.pallas_call(
        functools.partial(_hgrn_kernel, layer=layer, sub=sub, n_sub=n_sub, nseq=nseq, has_init=has_init, wide=wide,
                          hpb=hpb),
        grid=(nseq_total // nseq, hg, nblk),
        in_specs=in_specs,
        out_specs=out_specs,
        out_shape=out_shape,
        scratch_shapes=[pltpu.VMEM((hpb, HG_DV, HG_DK), F32)],
        compiler_params=pltpu.CompilerParams(
            dimension_semantics=("arbitrary", "arbitrary", "arbitrary"),
            vmem_limit_bytes=32 << 20),
        name="hgrn_prompt" if not has_init else "hgrn_sample",
    )(*args)


def _gmlp_kernel(gu_ref, vn_ref, w_ref, bias_ref, o_ref, *, n_chunks):
    for c in range(n_chunks):
        rs = slice(c * GM_CHUNK, (c + 1) * GM_CHUNK)
        vn = vn_ref[rs, :].astype(BF16)
        parts = []
        for g in range(GM_GROUPS):
            cs = slice(g * GM_GROUP_DIM, (g + 1) * GM_GROUP_DIM)
            parts.append(jnp.dot(w_ref[0, g], vn[:, cs], preferred_element_type=F32))
        mixed = jnp.concatenate(parts, axis=-1) + bias_ref[0]
        o_ref[rs, :] = (gu_ref[rs, :].astype(F32) * mixed).astype(o_ref.dtype)


def gmlp_mix(gu_act, vn, wmix, bias, *, tm, prompt_rows):
    m = gu_act.shape[0]
    first_sample_tile = prompt_rows // tm
    kind = lambda i: jnp.where(i >= first_sample_tile, 1, 0)
    return pl.pallas_call(
        functools.partial(_gmlp_kernel, n_chunks=tm // GM_CHUNK),
        grid=(m // tm,),
        in_specs=[
            pl.BlockSpec((tm, GM_WIDTH), lambda i: (i, 0)),
            pl.BlockSpec((tm, GM_WIDTH), lambda i: (i, 0)),
            pl.BlockSpec((1, GM_GROUPS, GM_CHUNK, GM_CHUNK), lambda i: (kind(i), 0, 0, 0)),
            pl.BlockSpec((1, GM_CHUNK, GM_WIDTH), lambda i: (kind(i), 0, 0)),
        ],
        out_specs=pl.BlockSpec((tm, GM_WIDTH), lambda i: (i, 0)),
        out_shape=jax.ShapeDtypeStruct((m, GM_WIDTH), BF16),
        compiler_params=pltpu.CompilerParams(dimension_semantics=("arbitrary",), vmem_limit_bytes=32 << 20),
        name="gmlp_mix",
    )(gu_act, vn, wmix, bias)


def _rope_tile(t, cos_t, sin_t):
    return t * cos_t + pltpu.roll(t, QK_ROPE, 1) * sin_t


def _epi_ffn_up(accs, row_aux, col_aux):
    return [jax.nn.silu(accs[0]) * accs[1]]


def _epi_small(accs, row_aux, col_aux):
    y = accs[0]
    cos_t, sin_t = row_aux
    qg, kvg = col_aux
    cqn = _rms(y[:, :Q_LORA], qg)
    ckvn = _rms(y[:, Q_LORA:Q_LORA + KV_LORA], kvg)
    roped = _rope_tile(y[:, Q_LORA + KV_LORA:], cos_t, sin_t)
    return [cqn, jnp.concatenate([ckvn, roped[:, :QK_ROPE]], axis=-1)]


def _epi_identity(accs, row_aux, col_aux):
    return [accs[0]]


def _epi_gelu(accs, row_aux, col_aux):
    return [jax.nn.gelu(accs[0])]


def _epi_gelu_ln(accs, row_aux, col_aux):
    y = jax.nn.gelu(accs[0])
    g, b = col_aux
    mu = jnp.mean(y, axis=-1, keepdims=True)
    var = jnp.mean(jnp.square(y - mu), axis=-1, keepdims=True)
    return [(y - mu) * lax.rsqrt(var + EPS) * g + b]


def _epi_sigmoid(accs, row_aux, col_aux):
    return [jax.nn.sigmoid(accs[0])]


def _epi_q_rope(accs, row_aux, col_aux):
    y = accs[0]
    cos_t, sin_t = row_aux
    parts = []
    for h in range(MLA_HEADS):
        base = h * Q_HEAD_PAD
        parts.append(y[:, base:base + QK_NOPE])
        parts.append(_rope_tile(y[:, base + QK_NOPE:base + Q_HEAD_PAD], cos_t, sin_t))
    return [jnp.concatenate(parts, axis=-1)]


def _epi_kv_split(accs, row_aux, col_aux):
    n_k = MLA_HEADS * Q_HEAD_PAD
    return [accs[0][:, :n_k], accs[0][:, n_k:]]


def _rot_cols(w):
    half = QK_ROPE // 2
    return jnp.concatenate([-w[..., half:], w[..., :half]], axis=-1)


def _prep_layer(l, w_in, w_uq, w_ukv, gmlp_w_s, gmlp_b_s, t_sample):
    wi = w_in[l]
    o_kr = Q_LORA + KV_LORA
    o_h = o_kr + QK_ROPE
    w_small = jnp.concatenate([wi[:, :o_h], _rot_cols(wi[:, o_kr:o_h])], axis=-1).astype(BF16)
    w_hgrn = wi[:, o_h:o_h + 4 * HG_WIDTH].astype(BF16)
    o_g = o_h + 4 * HG_WIDTH
    w_gu = wi[:, o_g:o_g + GM_WIDTH].astype(BF16)
    w_gv = wi[:, o_g + GM_WIDTH:o_g + 2 * GM_WIDTH].astype(BF16)
    w_gate = wi[:, o_g + 2 * GM_WIDTH:].astype(BF16)

    uq = w_uq[l].reshape(Q_LORA, MLA_HEADS, QK_NOPE + QK_ROPE)
    uq_rope = uq[..., QK_NOPE:]
    w_q = jnp.concatenate([uq[..., :QK_NOPE], uq_rope, _rot_cols(uq_rope)], axis=-1)
    w_q = w_q.reshape(Q_LORA, MLA_HEADS * Q_HEAD_PAD).astype(BF16)

    ukv = w_ukv[l].reshape(KV_LORA, MLA_HEADS, QK_NOPE + V_DIM)
    w_uk, w_uv = ukv[..., :QK_NOPE], ukv[..., QK_NOPE:]
    k_top = jnp.concatenate([w_uk, jnp.zeros((KV_LORA, MLA_HEADS, Q_HEAD_PAD - QK_NOPE), F32)], axis=-1)
    eye = jnp.concatenate([jnp.zeros((QK_ROPE, QK_NOPE), F32), jnp.eye(QK_ROPE, dtype=F32),
                           jnp.zeros((QK_ROPE, Q_HEAD_PAD - QK_NOPE - QK_ROPE), F32)], axis=-1)
    k_bot = jnp.broadcast_to(eye[:, None, :], (QK_ROPE, MLA_HEADS, Q_HEAD_PAD))
    w_k = jnp.concatenate([k_top, k_bot], axis=0).reshape(CACHE_DIM, MLA_HEADS * Q_HEAD_PAD).astype(BF16)
    w_v = jnp.concatenate([w_uv, jnp.zeros((QK_ROPE, MLA_HEADS, V_DIM), F32)], axis=0)
    w_v = w_v.reshape(CACHE_DIM, MLA_WIDTH).astype(BF16)
    w_lat = jnp.transpose(w_uk, (1, 2, 0)).astype(BF16)
    w_uvh = jnp.transpose(w_uv, (1, 0, 2)).astype(BF16)

    ws = gmlp_w_s[l]
    w_prompt = jnp.tril(ws)
    small = jnp.tril(ws[:, :t_sample, :t_sample])
    reps = GM_CHUNK // t_sample
    w_sample = jnp.einsum("ab,gts->gatbs", jnp.eye(reps, dtype=F32), small).reshape(GM_GROUPS, GM_CHUNK, GM_CHUNK)
    wmix = jnp.stack([w_prompt, w_sample]).astype(BF16)
    bs = gmlp_b_s[l]
    bias_p = jnp.repeat(bs.T[:, :, None], GM_GROUP_DIM, axis=2).reshape(GM_CHUNK, GM_WIDTH)
    bias_s = jnp.tile(jnp.repeat(bs[:, :t_sample].T[:, :, None], GM_GROUP_DIM, axis=2).reshape(t_sample, GM_WIDTH),
                      (reps, 1))
    bias = jnp.stack([bias_p, bias_s])
    return dict(w_small=w_small, w_hgrn=w_hgrn, w_gu=w_gu, w_gv=w_gv, w_gate=w_gate, w_q=w_q,
                w_kv=jnp.concatenate([w_k, w_v], axis=-1), w_lat=w_lat, w_uvh=w_uvh, wmix=wmix, bias=bias)


def _rope_tables(pos):
    half = QK_ROPE // 2
    inv = ROPE_THETA ** (-jnp.arange(half, dtype=F32) / half)
    ang = pos.astype(F32)[:, None] * inv[None, :]
    pad = jnp.zeros((pos.shape[0], LANES - QK_ROPE), F32)
    cos_t = jnp.concatenate([jnp.cos(ang), jnp.cos(ang), pad], axis=-1)
    sin_t = jnp.concatenate([jnp.sin(ang), jnp.sin(ang), pad], axis=-1)
    return cos_t, sin_t


def kernel(x_prompt, x_sample, cache_mla, state_hgrn, page_table, norm_gains, w_ffn1_gate, w_ffn1_up, w_ffn1_down, w_ffn2_gate, w_ffn2_up, w_ffn2_down, w_in, mla_q_norm, mla_kv_norm, w_uq, w_ukv, hgrn_lb_logits, hgrn_out_norm, gmlp_ln_g, gmlp_ln_b, gmlp_w_s, gmlp_b_s, w_br_mla, w_br_hgrn, w_br_gmlp, w_out):
    batch, seq, d = x_prompt.shape
    nb, t_s, _ = x_sample.shape
    m_p, m_s = batch * seq, nb * t_s
    m = m_p + m_s
    past = page_table.shape[1] * PAGE_SIZE
    tm = 1024
    while m_p % tm or m_s % tm:
        tm //= 2
    th = min(512, tm)
    tq4 = min(256, tm)
    assert tm >= GM_CHUNK and m_p % m_s == 0

    pos = jnp.concatenate([jnp.tile(jnp.arange(seq), batch), jnp.tile(past + jnp.arange(t_s), nb)])
    cos_t, sin_t = _rope_tables(pos)
    bf = lambda w: w.astype(BF16)
    cache_t = jnp.swapaxes(cache_mla, 2, 3)

    rows_all, s_p_all, s_s_all, vn_all = [], [], [], []
    x, xn = join_rmsnorm(x_prompt.reshape(m_p, d), x_sample.reshape(m_s, d), norm_gains[0, 0], tm=th)
    for l in range(DEPTH):
        ng = norm_gains[l]
        p = _prep_layer(l, w_in, w_uq, w_ukv, gmlp_w_s, gmlp_b_s, t_s)

        hmid = mm(xn, [w_ffn1_gate, w_ffn1_up], _epi_ffn_up, [(512, BF16)], tm=tm, tn=512, layer=l,
                  name="ffn1_up")[0]
        x, hn = mm_res(hmid, bf(w_ffn1_down[l]), x, ng[1], ng[2], alpha=0.5, tm=tq4, name="ffn1_down")

        cqn, rows = mm(hn, [p["w_small"]], _epi_small, [(Q_LORA, BF16), (CACHE_DIM, F32)], tm=th,
                       tn=p["w_small"].shape[1], row_aux=(cos_t, sin_t),
                       col_aux=(mla_q_norm[l].reshape(1, -1), mla_kv_norm[l].reshape(1, -1)), name="in_proj_mla")
        hproj = mm(hn, [p["w_hgrn"]], _epi_identity, [(1024, F32)], tm=tm, tn=1024, name="in_proj_hgrn")[0]
        gu_act = mm(hn, [p["w_gu"]], _epi_gelu, [(GM_WIDTH, BF16)], tm=tm, tn=GM_WIDTH, name="in_proj_gu")[0]
        vn = mm(hn, [p["w_gv"]], _epi_gelu_ln, [(GM_WIDTH, F32)], tm=tm, tn=GM_WIDTH,
                col_aux=(gmlp_ln_g[l].reshape(1, -1), gmlp_ln_b[l].reshape(1, -1)), name="in_proj_gv")[0]
        gate = mm(hn, [p["w_gate"]], _epi_sigmoid, [(1024, BF16)], tm=tm, tn=1024, name="in_proj_gate")[0]

        q = mm(cqn, [p["w_q"]], _epi_q_rope, [(MLA_HEADS * Q_HEAD_PAD, BF16)], tm=th, tn=MLA_HEADS * Q_HEAD_PAD,
               row_aux=(cos_t, sin_t), name="mla_q")[0]
        kfull, vv = mm(rows, [p["w_kv"]], _epi_kv_split, [(MLA_HEADS * Q_HEAD_PAD, BF16), (MLA_WIDTH, BF16)],
                       tm=th, tn=p["w_kv"].shape[1], rows=m_p, name="mla_kv")
        a_p = mla_prompt_attention(q, kfull, vv, batch=batch, seq=seq, tq=min(1024, seq), hpb=2)

        qcat = sample_query_latent(q, p["w_lat"], row0=m_p, nb=nb, t=t_s)
        o_lat = mla_sample_attention(qcat.reshape(nb, MLA_HEADS * t_s, CACHE_DIM), rows[m_p:].reshape(nb, t_s, CACHE_DIM),
                                     cache_t, page_table, layer=l, group=32)
        a_s = sample_value_proj(o_lat.reshape(nb, MLA_HEADS, t_s, KV_LORA), p["w_uvh"], nb=nb, t=t_s)

        b_p, s_p = hgrn_mixer(hproj, hgrn_lb_logits, hgrn_out_norm[l], None, layer=l, row0=0, nseq_total=batch,
                              seq_len=seq, blk_len=256, nseq=1, sub=HGRN_SUB, wide=HGRN_WIDE, hpb=4)
        b_s, s_s = hgrn_mixer(hproj, hgrn_lb_logits, hgrn_out_norm[l], state_hgrn, layer=l, row0=m_p,
                              nseq_total=nb, seq_len=t_s, blk_len=t_s, nseq=8, sub=t_s, hpb=4)

        cmix = gmlp_mix(gu_act, vn, p["wmix"], p["bias"], tm=th, prompt_rows=m_p)

        merged = merge_branches(a_p, a_s, b_p, b_s, cmix, gate, bf(w_br_mla[l]), bf(w_br_hgrn[l]),
                                bf(w_br_gmlp[l]), tm=th)
        x, hn4 = mm_res(merged, bf(w_out[l]), x, ng[3], ng[4], alpha=1.0, tm=th, name="merge_out")

        hmid = mm(hn4, [w_ffn2_gate, w_ffn2_up], _epi_ffn_up, [(512, BF16)], tm=tm, tn=512, layer=l,
                  name="ffn2_up")[0]
        g_next = norm_gains[l + 1, 0] if l + 1 < DEPTH else None
        x, xn = mm_res(hmid, bf(w_ffn2_down[l]), x, ng[5], g_next, alpha=0.5, tm=tq4, name="ffn2_down")

        rows_all.append(rows)
        s_p_all.append(s_p)
        s_s_all.append(s_s)
        vn_all.append(vn[m_p:])

    rows_st = jnp.stack(rows_all)
    return (x[:m_p].reshape(batch, seq, d), x[m_p:].reshape(nb, t_s, d),
            rows_st[:, :m_p].reshape(DEPTH, batch, seq, CACHE_DIM), rows_st[:, m_p:].reshape(DEPTH, nb, t_s, CACHE_DIM),
            jnp.stack(s_p_all), jnp.stack(s_s_all), jnp.stack(vn_all).reshape(DEPTH, nb, t_s, GM_WIDTH))
```

```python
import functools

import jax
import jax.numpy as jnp
import numpy as np
from jax import lax
from jax.experimental import pallas as pl
from jax.experimental.pallas import tpu as pltpu

D_MODEL = 2048
DEPTH = 2
PAGE_SIZE = 128
MLA_HEADS = 8
QK_NOPE = 128
QK_ROPE = 64
V_DIM = 128
Q_LORA = 512
KV_LORA = 256
ROPE_THETA = 10000.0
MLA_SCALE = (QK_NOPE + QK_ROPE) ** -0.5
MLA_WIDTH = MLA_HEADS * V_DIM
CACHE_DIM = KV_LORA + QK_ROPE
HG_HEADS = 8
HG_DK = 128
HG_DV = 128
HG_WIDTH = HG_HEADS * HG_DV
GM_GROUPS = 8
GM_GROUP_DIM = 128
GM_CHUNK = 128
GM_WIDTH = GM_GROUPS * GM_GROUP_DIM
D_FF = 5632
N_BRANCH = 3
EPS = 1e-6

LANES = 128
Q_HEAD_PAD = 2 * LANES
VMEM_CAP_BYTES = 60000 * 1024
DEC_STREAMS = 4
DEC_AHEAD = 3
HGRN_SUB = 16
HGRN_WIDE = 64
HGRN_SAFE_LOG = 80.0

F32 = jnp.float32
BF16 = jnp.bfloat16


def _vmem_limit(block_bytes, temp_bytes=0):
    est = 2 * sum(block_bytes) + temp_bytes + (4 << 20)
    return int(min(max(est, 16 << 20), VMEM_CAP_BYTES))


def _nbytes(shape, dtype):
    return int(np.prod(shape)) * jnp.dtype(dtype).itemsize


def _rms(y, g):
    return y * lax.rsqrt(jnp.mean(y * y, axis=-1, keepdims=True) + EPS) * g


def _rmsnorm_kernel(xa_ref, xb_ref, g_ref, x_ref, o_ref, *, tiles_a):
    @pl.when(pl.program_id(0) < tiles_a)
    def _():
        x_ref[...] = xa_ref[...]

    @pl.when(pl.program_id(0) >= tiles_a)
    def _():
        x_ref[...] = xb_ref[...]

    o_ref[...] = _rms(x_ref[...], g_ref[...]).astype(o_ref.dtype)


def join_rmsnorm(xa, xb, g, *, tm):
    (ma, d), mb = xa.shape, xb.shape[0]
    assert ma % tm == 0 and mb % tm == 0
    ta = ma // tm
    m = ma + mb
    tile = lambda: pl.BlockSpec((tm, d), lambda i: (i, 0))
    return pl.pallas_call(
        functools.partial(_rmsnorm_kernel, tiles_a=ta),
        grid=(m // tm,),
        in_specs=[pl.BlockSpec((tm, d), lambda i: (jnp.minimum(i, ta - 1), 0)),
                  pl.BlockSpec((tm, d), lambda i: (jnp.maximum(i - ta, 0), 0)),
                  pl.BlockSpec((1, d), lambda i: (0, 0))],
        out_specs=[tile(), tile()],
        out_shape=[jax.ShapeDtypeStruct((m, d), F32), jax.ShapeDtypeStruct((m, d), BF16)],
        compiler_params=pltpu.CompilerParams(
            dimension_semantics=("arbitrary",),
            vmem_limit_bytes=_vmem_limit([3 * _nbytes((tm, d), F32), _nbytes((tm, d), BF16)], _nbytes((tm, d), F32))),
        name="join_rmsnorm",
    )(xa, xb, g.reshape(1, d))


def _mm_kernel(*refs, n_w, n_row, n_col, n_out, epi, cast_w):
    x_ref = refs[0]
    w_refs = refs[1:1 + n_w]
    row_refs = refs[1 + n_w:1 + n_w + n_row]
    col_refs = refs[1 + n_w + n_row:1 + n_w + n_row + n_col]
    out_refs = refs[1 + n_w + n_row + n_col:1 + n_w + n_row + n_col + n_out]
    if cast_w:
        wb_refs = refs[len(refs) - n_w:]

        @pl.when(pl.program_id(1) == 0)
        def _():
            for w, wb in zip(w_refs, wb_refs):
                wb[...] = w[...].astype(BF16)

        w_refs = wb_refs
    x = x_ref[...].astype(BF16)
    accs = [jnp.dot(x, w[...], preferred_element_type=F32) for w in w_refs]
    outs = epi(accs, [r[...] for r in row_refs], [c[...] for c in col_refs])
    for o_ref, o in zip(out_refs, outs):
        o_ref[...] = o.astype(o_ref.dtype)


def mm(x, ws, epi, outs, *, tm, tn, rows=None, row_aux=(), col_aux=(), layer=None, name):
    m = rows if rows is not None else x.shape[0]
    k = x.shape[1]
    n = ws[0].shape[-1]
    nt = n // tn
    cast_w = layer is not None
    assert m % tm == 0 and n % tn == 0
    in_specs = [pl.BlockSpec((tm, k), lambda j, i: (i, 0))]
    if cast_w:
        in_specs += [pl.BlockSpec((None, k, tn), lambda j, i: (layer, 0, j)) for _ in ws]
    else:
        in_specs += [pl.BlockSpec((k, tn), lambda j, i: (0, j)) for _ in ws]
    in_specs += [pl.BlockSpec((tm, a.shape[1]), lambda j, i: (i, 0)) for a in row_aux]
    in_specs += [pl.BlockSpec((1, a.shape[1] // nt), lambda j, i: (0, j)) for a in col_aux]
    out_specs = [pl.BlockSpec((tm, w), lambda j, i: (i, j)) for w, _ in outs]
    out_shape = [jax.ShapeDtypeStruct((m, w * nt), dt) for w, dt in outs]
    blocks = [_nbytes((tm, k), x.dtype)] + [_nbytes((k, tn), ws[0].dtype)] * len(ws)
    blocks += [_nbytes((tm, a.shape[1]), a.dtype) for a in row_aux]
    blocks += [_nbytes((tm, w), dt) for w, dt in outs]
    scratch = [pltpu.VMEM((k, tn), BF16) for _ in ws] if cast_w else []
    res = pl.pallas_call(
        functools.partial(_mm_kernel, n_w=len(ws), n_row=len(row_aux), n_col=len(col_aux), n_out=len(outs), epi=epi,
                          cast_w=cast_w),
        grid=(nt, m // tm),
        in_specs=in_specs,
        out_specs=out_specs,
        out_shape=out_shape,
        scratch_shapes=scratch,
        compiler_params=pltpu.CompilerParams(
            dimension_semantics=("arbitrary", "arbitrary"),
            vmem_limit_bytes=_vmem_limit(blocks, (3 * _nbytes((tm, tn), F32) + cast_w * _nbytes((k, tn), BF16))
                                         * len(ws))),
        name=name,
    )(x, *ws, *row_aux, *col_aux)
    return res


def _mm_res_kernel(*refs, alpha, has_next):
    if has_next:
        h_ref, w_ref, x_ref, gp_ref, gn_ref, xo_ref, no_ref = refs
    else:
        h_ref, w_ref, x_ref, gp_ref, xo_ref = refs
    y = jnp.dot(h_ref[...], w_ref[...], preferred_element_type=F32)
    xo = x_ref[...] + alpha * _rms(y, gp_ref[...])
    xo_ref[...] = xo
    if has_next:
        no_ref[...] = _rms(xo, gn_ref[...]).astype(no_ref.dtype)


def mm_res(h, w, x, g_post, g_next, *, alpha, tm, name):
    m, kdim = h.shape
    d = w.shape[1]
    assert m % tm == 0
    has_next = g_next is not None
    once = pl.Buffered(1)
    in_specs = [
        pl.BlockSpec((tm, kdim), lambda i: (i, 0)),
        pl.BlockSpec((kdim, d), lambda i: (0, 0), pipeline_mode=once),
        pl.BlockSpec((tm, d), lambda i: (i, 0)),
        pl.BlockSpec((1, d), lambda i: (0, 0)),
    ]
    args = [h, w, x, g_post.reshape(1, d)]
    out_specs = [pl.BlockSpec((tm, d), lambda i: (i, 0))]
    out_shape = [jax.ShapeDtypeStruct((m, d), F32)]
    streamed = [_nbytes((tm, kdim), BF16), 2 * _nbytes((tm, d), F32)]
    if has_next:
        in_specs.append(pl.BlockSpec((1, d), lambda i: (0, 0)))
        args.append(g_next.reshape(1, d))
        out_specs.append(pl.BlockSpec((tm, d), lambda i: (i, 0)))
        out_shape.append(jax.ShapeDtypeStruct((m, d), BF16))
        streamed.append(_nbytes((tm, d), BF16))
    res = pl.pallas_call(
        functools.partial(_mm_res_kernel, alpha=alpha, has_next=has_next),
        grid=(m // tm,),
        in_specs=in_specs,
        out_specs=out_specs,
        out_shape=out_shape,
        compiler_params=pltpu.CompilerParams(
            dimension_semantics=("arbitrary",),
            vmem_limit_bytes=_vmem_limit(streamed, _nbytes((kdim, d), BF16) + 3 * _nbytes((tm, d), F32))),
        name=name,
    )(*args)
    return (res[0], res[1]) if has_next else (res[0], None)


def _merge_kernel(ap_ref, as_ref, bp_ref, bs_ref, c_ref, g_ref, wa_ref, wb_ref, wc_ref, o_ref, a_ref, b_ref, *,
                  d, tiles_p):
    @pl.when(pl.program_id(0) < tiles_p)
    def _():
        a_ref[...] = ap_ref[...]
        b_ref[...] = bp_ref[...]

    @pl.when(pl.program_id(0) >= tiles_p)
    def _():
        a_ref[...] = as_ref[...]
        b_ref[...] = bs_ref[...]

    g = g_ref[...].astype(F32)
    acc = g[:, :d] * jnp.dot(a_ref[...], wa_ref[...], preferred_element_type=F32)
    acc += g[:, d:2 * d] * jnp.dot(b_ref[...], wb_ref[...], preferred_element_type=F32)
    acc += g[:, 2 * d:] * jnp.dot(c_ref[...], wc_ref[...], preferred_element_type=F32)
    o_ref[...] = acc.astype(o_ref.dtype)


def merge_branches(a_p, a_s, b_p, b_s, c, g, wa, wb, wc, *, tm):
    (m_p, kw), m_s = a_p.shape, a_s.shape[0]
    assert m_p % tm == 0 and m_s % tm == 0
    tp = m_p // tm
    m = m_p + m_s
    d = wa.shape[1]
    row = lambda w: pl.BlockSpec((tm, w), lambda i: (i, 0))
    seg_p = pl.BlockSpec((tm, kw), lambda i: (jnp.minimum(i, tp - 1), 0))
    seg_s = pl.BlockSpec((tm, kw), lambda i: (jnp.maximum(i - tp, 0), 0))
    full = pl.BlockSpec((kw, d), lambda i: (0, 0))
    blocks = [5 * _nbytes((tm, kw), BF16), _nbytes((tm, 3 * d), BF16), 3 * _nbytes((kw, d), BF16), _nbytes((tm, d), BF16)]
    return pl.pallas_call(
        functools.partial(_merge_kernel, d=d, tiles_p=tp),
        grid=(m // tm,),
        in_specs=[seg_p, seg_s, seg_p, seg_s, row(kw), row(3 * d), full, full, full],
        out_specs=row(d),
        out_shape=jax.ShapeDtypeStruct((m, d), BF16),
        scratch_shapes=[pltpu.VMEM((tm, kw), BF16), pltpu.VMEM((tm, kw), BF16)],
        compiler_params=pltpu.CompilerParams(
            dimension_semantics=("arbitrary",),
            vmem_limit_bytes=_vmem_limit(blocks, 4 * _nbytes((tm, d), F32) + 2 * _nbytes((tm, kw), BF16))),
        name="merge_branches",
    )(a_p, a_s, b_p, b_s, c, g, wa, wb, wc)


def _flash_kernel(q_ref, k_ref, v_ref, o_ref, *, tq, scale, hpb):
    qi = pl.program_id(2)
    c2 = scale * np.log2(np.e)
    qs = [q_ref[:, h * Q_HEAD_PAD:(h + 1) * Q_HEAD_PAD] for h in range(hpb)]

    def step(kb, carry, masked):
        start = pl.multiple_of(kb * tq, tq)
        new = []
        for h in range(hpb):
            m_prev, l_prev, acc = carry[h]
            ks = k_ref[pl.ds(start, tq), h * Q_HEAD_PAD:(h + 1) * Q_HEAD_PAD]
            vs = v_ref[pl.ds(start, tq), h * V_DIM:(h + 1) * V_DIM]
            s = lax.dot_general(qs[h], ks, (((1,), (1,)), ((), ())), preferred_element_type=F32)
            if masked:
                r = lax.broadcasted_iota(jnp.int32, (tq, tq), 0)
                c = lax.broadcasted_iota(jnp.int32, (tq, tq), 1)
                s = jnp.where(c <= r, s, -jnp.inf)
            m_new = jnp.maximum(m_prev, jnp.max(s, axis=-1, keepdims=True))
            a = jnp.exp2((m_prev - m_new) * c2)
            p = jnp.exp2((s - m_new) * c2)
            l_new = a * l_prev + jnp.sum(p, axis=-1, keepdims=True)
            acc = a * acc + jnp.dot(p.astype(BF16), vs, preferred_element_type=F32)
            new.append((m_new, l_new, acc))
        return tuple(new)

    one = (jnp.full((tq, 1), -jnp.inf, F32), jnp.zeros((tq, 1), F32), jnp.zeros((tq, V_DIM), F32))
    carry = lax.fori_loop(0, qi, lambda kb, c: step(kb, c, False), (one,) * hpb)
    final = step(qi, carry, True)
    o_ref[...] = jnp.concatenate([acc / l_fin for _, l_fin, acc in final], axis=-1).astype(o_ref.dtype)


def mla_prompt_attention(q, kfull, v, *, batch, seq, tq, hpb):
    nq = seq // tq
    assert MLA_HEADS % hpb == 0
    blocks = [_nbytes((tq, hpb * Q_HEAD_PAD), BF16), _nbytes((seq, hpb * Q_HEAD_PAD), BF16),
              _nbytes((seq, hpb * V_DIM), BF16), _nbytes((tq, hpb * V_DIM), BF16)]
    return pl.pallas_call(
        functools.partial(_flash_kernel, tq=tq, scale=MLA_SCALE, hpb=hpb),
        grid=(batch, MLA_HEADS // hpb, nq),
        in_specs=[
            pl.BlockSpec((tq, hpb * Q_HEAD_PAD), lambda b, h, i: (b * nq + i, h)),
            pl.BlockSpec((seq, hpb * Q_HEAD_PAD), lambda b, h, i: (b, h)),
            pl.BlockSpec((seq, hpb * V_DIM), lambda b, h, i: (b, h)),
        ],
        out_specs=pl.BlockSpec((tq, hpb * V_DIM), lambda b, h, i: (b * nq + i, h)),
        out_shape=jax.ShapeDtypeStruct((batch * seq, MLA_WIDTH), BF16),
        compiler_params=pltpu.CompilerParams(
            dimension_semantics=("arbitrary", "arbitrary", "arbitrary"),
            vmem_limit_bytes=_vmem_limit(blocks, 6 * hpb * _nbytes((tq, tq), F32))),
        name="mla_prompt_attention",
    )(q, kfull, v)


def _dec_kernel(pt_ref, q_ref, new_ref, cache_ref, o_ref, buf, sem, *, layer, group, n_groups, t_new, scale):
    b = pl.program_id(0)
    rows_q = q_ref.shape[1]

    def page_copy(bb, g, p):
        page = pt_ref[bb, g * group + p]
        return pltpu.make_async_copy(cache_ref.at[layer, page], buf.at[g, p], sem.at[g])

    def start(bb, g):
        for p in range(group):
            page_copy(bb, g, p).start(priority=p % 2)

    def wait(bb, g):
        for p in range(group):
            page_copy(bb, g, p).wait()

    @pl.when(b == 0)
    def _():
        for g in range(DEC_AHEAD):
            start(0, g)

    def fold(state, s, vals, vals_dim):
        m_run, l_run, acc = state
        m_new = jnp.maximum(m_run, jnp.max(s, axis=-1, keepdims=True))
        a = jnp.exp(m_run - m_new)
        p = jnp.exp(s - m_new)
        l_new = a * l_run + jnp.sum(p, axis=-1, keepdims=True)
        acc = a * acc + lax.dot_general(p.astype(BF16), vals, (((1,), (vals_dim,)), ((), ())),
                                        preferred_element_type=F32)
        return m_new, l_new, acc

    q = q_ref[0].astype(BF16)
    empty = (jnp.full((rows_q, 1), -jnp.inf, F32), jnp.zeros((rows_q, 1), F32), jnp.zeros((rows_q, KV_LORA), F32))
    states = [empty] * DEC_STREAMS
    span = group * PAGE_SIZE // DEC_STREAMS
    for g in range(n_groups):
        ahead = g + DEC_AHEAD
        if ahead < n_groups:
            start(b, ahead)
        else:
            @pl.when(b + 1 < pl.num_programs(0))
            def _():
                start(b + 1, ahead - n_groups)
        wait(b, g)
        for c in range(DEC_STREAMS):
            pages = range(c * span // PAGE_SIZE, (c + 1) * span // PAGE_SIZE)
            keys_t = jnp.concatenate([buf[g, p] for p in pages], axis=1).astype(BF16)
            s = jnp.dot(q, keys_t, preferred_element_type=F32) * scale
            states[c] = fold(states[c], s, keys_t[:KV_LORA], 1)
    keys = new_ref[0].astype(BF16)
    s = lax.dot_general(q, keys, (((1,), (1,)), ((), ())), preferred_element_type=F32) * scale
    t_of_row = lax.broadcasted_iota(jnp.int32, (rows_q, t_new), 0) % t_new
    j = lax.broadcasted_iota(jnp.int32, (rows_q, t_new), 1)
    s = jnp.where(j <= t_of_row, s, -jnp.inf)
    states[0] = fold(states[0], s, keys[:, :KV_LORA], 0)
    m_all = functools.reduce(jnp.maximum, [st[0] for st in states])
    l_fin = sum(st[1] * jnp.exp(st[0] - m_all) for st in states)
    acc = sum(st[2] * jnp.exp(st[0] - m_all) for st in states)
    o_ref[0] = acc / l_fin


def mla_sample_attention(qcat, rows_new, cache_t, page_table, *, layer, group):
    nb, rows_q, _ = qcat.shape
    t_new = rows_new.shape[1]
    n_pages = page_table.shape[1]
    n_groups = n_pages // group
    assert n_pages % group == 0 and DEC_AHEAD < n_groups
    grid_spec = pltpu.PrefetchScalarGridSpec(
        num_scalar_prefetch=1,
        grid=(nb,),
        in_specs=[
            pl.BlockSpec((1, rows_q, CACHE_DIM), lambda b, pt: (b, 0, 0)),
            pl.BlockSpec((1, t_new, CACHE_DIM), lambda b, pt: (b, 0, 0)),
            pl.BlockSpec(memory_space=pl.ANY),
        ],
        out_specs=pl.BlockSpec((1, rows_q, KV_LORA), lambda b, pt: (b, 0, 0)),
        scratch_shapes=[
            pltpu.VMEM((n_groups, group, CACHE_DIM, PAGE_SIZE), F32),
            pltpu.SemaphoreType.DMA((n_groups,)),
        ],
    )
    scratch_bytes = (_nbytes((n_groups, CACHE_DIM, group * PAGE_SIZE), F32)
                     + _nbytes((CACHE_DIM, group * PAGE_SIZE), F32)
                     + 6 * _nbytes((rows_q, group * PAGE_SIZE), F32))
    return pl.pallas_call(
        functools.partial(_dec_kernel, layer=layer, group=group, n_groups=n_groups, t_new=t_new, scale=MLA_SCALE),
        grid_spec=grid_spec,
        out_shape=jax.ShapeDtypeStruct((nb, rows_q, KV_LORA), F32),
        compiler_params=pltpu.CompilerParams(
            dimension_semantics=("arbitrary",),
            vmem_limit_bytes=_vmem_limit([_nbytes((rows_q, CACHE_DIM), F32)], scratch_bytes)),
        name="mla_sample_attention",
    )(page_table, qcat, rows_new, cache_t)


def _qlat_kernel(q_ref, w_ref, o_ref, *, nb, t):
    q = q_ref[...]
    lat = jnp.dot(q[:, :QK_NOPE], w_ref[0], preferred_element_type=F32)
    cat = jnp.concatenate([lat, q[:, QK_NOPE:QK_NOPE + QK_ROPE].astype(F32)], axis=-1)
    o_ref[...] = cat.reshape(nb, 1, t, CACHE_DIM).astype(o_ref.dtype)


def sample_query_latent(q, w_lat, *, row0, nb, t):
    m_s = nb * t
    blk0 = row0 // m_s
    return pl.pallas_call(
        functools.partial(_qlat_kernel, nb=nb, t=t),
        grid=(MLA_HEADS,),
        in_specs=[
            pl.BlockSpec((m_s, Q_HEAD_PAD), lambda h: (blk0, h)),
            pl.BlockSpec((1, QK_NOPE, KV_LORA), lambda h: (h, 0, 0)),
        ],
        out_specs=pl.BlockSpec((nb, 1, t, CACHE_DIM), lambda h: (0, h, 0, 0)),
        out_shape=jax.ShapeDtypeStruct((nb, MLA_HEADS, t, CACHE_DIM), F32),
        compiler_params=pltpu.CompilerParams(dimension_semantics=("arbitrary",)),
        name="sample_query_latent",
    )(q, w_lat)


def _oproj_kernel(o_ref, w_ref, a_ref, *, nb, t):
    o = o_ref[...].reshape(nb * t, KV_LORA).astype(BF16)
    a_ref[...] = jnp.dot(o, w_ref[0], preferred_element_type=F32).astype(a_ref.dtype)


def sample_value_proj(o_lat, w_uv, *, nb, t):
    return pl.pallas_call(
        functools.partial(_oproj_kernel, nb=nb, t=t),
        grid=(MLA_HEADS,),
        in_specs=[
            pl.BlockSpec((nb, 1, t, KV_LORA), lambda h: (0, h, 0, 0)),
            pl.BlockSpec((1, KV_LORA, V_DIM), lambda h: (h, 0, 0)),
        ],
        out_specs=pl.BlockSpec((nb * t, V_DIM), lambda h: (0, h)),
        out_shape=jax.ShapeDtypeStruct((nb * t, MLA_WIDTH), BF16),
        compiler_params=pltpu.CompilerParams(dimension_semantics=("arbitrary",)),
        name="sample_value_proj",
    )(o_lat, w_uv)


def _hgrn_kernel(*refs, layer, sub, n_sub, nseq, has_init, wide, hpb):
    if has_init:
        lbl_ref, hq_ref, hf_ref, hi_ref, hg_ref, gain_ref, s0_ref, b_ref, sout_ref, st_ref = refs
    else:
        lbl_ref, hq_ref, hf_ref, hi_ref, hg_ref, gain_ref, b_ref, sout_ref, st_ref = refs
    rows = nseq * n_sub * sub
    logits = lbl_ref[...]
    e = jnp.exp(logits - jnp.max(logits, axis=0, keepdims=True))
    wsm = e / jnp.sum(e, axis=0, keepdims=True)
    lb = jnp.sum(wsm[:layer + 1], axis=0, keepdims=True) - wsm[0:1]

    f = lb + (1.0 - lb) * jax.nn.sigmoid(hf_ref[...])
    logf = jnp.log(f)
    kk = 1.0 - f
    q = jax.nn.silu(hq_ref[...])
    v = hi_ref[...]
    row = lax.broadcasted_iota(jnp.int32, (rows, hpb * HG_DK), 0)

    def block_cumsum(width):
        in_blk = jnp.bitwise_and(row, width - 1)
        acc = logf
        shift = 1
        while shift < width:
            acc = acc + jnp.where(in_blk >= shift, pltpu.roll(acc, shift, 0), 0.0)
            shift *= 2
        return acc

    def lower_tri(width):
        return (lax.broadcasted_iota(jnp.int32, (width, width), 1)
                <= lax.broadcasted_iota(jnp.int32, (width, width), 0))

    def head(x, h):
        return x[:, h * HG_DK:(h + 1) * HG_DK]

    def exact_blocks(sts, lo, hi):
        bcum = block_cumsum(sub)
        qt = q * jnp.exp(bcum)
        tri = lower_tri(sub)
        sts = list(sts)
        outs = [[] for _ in range(hpb)]
        for r0 in range(lo, hi, sub):
            sl = slice(r0, r0 + sub)
            for h in range(hpb):
                bj, qj, kj = head(bcum[sl], h), head(q[sl], h), head(kk[sl], h)
                vj = head(v[sl], h).astype(BF16)
                o_inter = lax.dot_general(head(qt[sl], h).astype(BF16), sts[h].astype(BF16),
                                          (((1,), (1,)), ((), ())), preferred_element_type=F32)
                diff = jnp.minimum(bj[:, None, :] - bj[None, :, :], 0.0)
                att = jnp.sum(qj[:, None, :] * kj[None, :, :] * jnp.exp(diff), axis=-1)
                att = jnp.where(tri, att, 0.0)
                o_intra = jnp.dot(att.astype(BF16), vj, preferred_element_type=F32)
                bl = bj[sub - 1:sub, :]
                kt = (kj * jnp.exp(bl - bj)).astype(BF16)
                upd = lax.dot_general(vj, kt, (((0,), (0,)), ((), ())), preferred_element_type=F32)
                sts[h] = sts[h] * jnp.exp(bl) + upd
                outs[h].append(o_intra + o_inter)
        return tuple(jnp.concatenate(o_h, axis=0) for o_h in outs), tuple(sts)

    def factored_blocks(sts, bcum, lo, hi):
        qt = q * jnp.exp(bcum)
        kt = kk * jnp.exp(-bcum)
        tri = lower_tri(wide)
        sts = list(sts)
        outs = [[] for _ in range(hpb)]
        for r0 in range(lo, hi, wide):
            sl = slice(r0, r0 + wide)
            bl_all = bcum[r0 + wide - 1:r0 + wide, :]
            kl_all = kk[sl] * jnp.exp(bl_all - bcum[sl])
            for h in range(hpb):
                qj, vj = head(qt[sl], h).astype(BF16), head(v[sl], h).astype(BF16)
                att = lax.dot_general(qj, head(kt[sl], h).astype(BF16), (((1,), (1,)), ((), ())),
                                      preferred_element_type=F32)
                att = jnp.where(tri, att, 0.0).astype(BF16)
                o_blk = jnp.dot(att, vj, preferred_element_type=F32)
                o_blk += lax.dot_general(qj, sts[h].astype(BF16), (((1,), (1,)), ((), ())),
                                         preferred_element_type=F32)
                upd = lax.dot_general(vj, head(kl_all, h).astype(BF16), (((0,), (0,)), ((), ())),
                                      preferred_element_type=F32)
                sts[h] = sts[h] * jnp.exp(head(bl_all, h)) + upd
                outs[h].append(o_blk)
        return tuple(jnp.concatenate(o_h, axis=0) for o_h in outs), tuple(sts)

    if not has_init:
        @pl.when(pl.program_id(2) == 0)
        def _():
            st_ref[...] = jnp.zeros_like(st_ref)

    per_seq = n_sub * sub
    outs = []
    for s in range(nseq):
        st0 = tuple(s0_ref[s, h].T if has_init else st_ref[h] for h in range(hpb))
        lo, hi = s * per_seq, (s + 1) * per_seq
        if wide is None:
            o_s, sts = exact_blocks(st0, lo, hi)
        else:
            bwide = block_cumsum(wide)
            safe = jnp.min(bwide) >= -HGRN_SAFE_LOG
            o_s, sts = lax.cond(safe, lambda s_in: factored_blocks(s_in, bwide, lo, hi),
                                lambda s_in: exact_blocks(s_in, lo, hi), st0)
        outs.append(o_s)
        for h in range(hpb):
            if has_init:
                sout_ref[s, h] = sts[h].T
            else:
                st_ref[h] = sts[h]
        if not has_init:
            @pl.when(pl.program_id(2) == pl.num_programs(2) - 1)
            def _():
                for h in range(hpb):
                    sout_ref[0, h] = sts[h].T
    gain = gain_ref[...]
    normed = []
    for h in range(hpb):
        o_h = outs[0][h] if nseq == 1 else jnp.concatenate([o_s[h] for o_s in outs], axis=0)
        normed.append(_rms(o_h, head(gain, h)))
    o = normed[0] if hpb == 1 else jnp.concatenate(normed, axis=-1)
    b_ref[...] = (o * jax.nn.silu(hg_ref[...])).astype(b_ref.dtype)


def hgrn_mixer(hproj, lb_logits, gain, state0, *, layer, row0, nseq_total, seq_len, blk_len, nseq, sub, wide=None,
               hpb=1):
    has_init = state0 is not None
    n_sub = blk_len // sub
    nblk = seq_len // blk_len
    rows = nseq * blk_len
    assert row0 % rows == 0 and nseq_total % nseq == 0 and (nseq == 1 or nblk == 1)
    r0 = row0 // rows
    assert HG_HEADS % hpb == 0
    hg = HG_HEADS // hpb
    wcol = hpb * HG_DK
    col = lambda p: pl.BlockSpec((rows, wcol), lambda s, h, c: (r0 + s * nblk + c, p * hg + h))
    in_specs = [
        pl.BlockSpec((DEPTH, wcol), lambda s, h, c: (0, h)),
        col(0), col(1), col(2), col(3),
        pl.BlockSpec((1, wcol), lambda s, h, c: (0, h)),
    ]
    args = [lb_logits, hproj, hproj, hproj, hproj, gain.reshape(1, HG_WIDTH)]
    if has_init:
        in_specs.append(pl.BlockSpec((None, nseq, hpb, HG_DK, HG_DV), lambda s, h, c: (layer, s, h, 0, 0)))
        args.append(state0)
    out_specs = [
        pl.BlockSpec((rows, wcol), lambda s, h, c: (s * nblk + c, h)),
        pl.BlockSpec((nseq, hpb, HG_DK, HG_DV), lambda s, h, c: (s, h, 0, 0)),
    ]
    out_shape = [
        jax.ShapeDtypeStruct((nseq_total * seq_len, HG_WIDTH), BF16),
        jax.ShapeDtypeStruct((nseq_total, HG_HEADS, HG_DK, HG_DV), F32),
    ]
    return pl.pallas_call(
        functools.partial(_hgrn_kernel, layer=layer, sub=sub, n_sub=n_sub, nseq=nseq, has_init=has_init, wide=wide,
                          hpb=hpb),
        grid=(nseq_total // nseq, hg, nblk),
        in_specs=in_specs,
        out_specs=out_specs,
        out_shape=out_shape,
        scratch_shapes=[pltpu.VMEM((hpb, HG_DV, HG_DK), F32)],
        compiler_params=pltpu.CompilerParams(
            dimension_semantics=("arbitrary", "arbitrary", "arbitrary"),
            vmem_limit_bytes=32 << 20),
        name="hgrn_prompt" if not has_init else "hgrn_sample",
    )(*args)


def _gmlp_kernel(gu_ref, vn_ref, w_ref, bias_ref, o_ref, *, n_chunks):
    for c in range(n_chunks):
        rs = slice(c * GM_CHUNK, (c + 1) * GM_CHUNK)
        vn = vn_ref[rs, :].astype(BF16)
        parts = []
        for g in range(GM_GROUPS):
            cs = slice(g * GM_GROUP_DIM, (g + 1) * GM_GROUP_DIM)
            parts.append(jnp.dot(w_ref[0, g], vn[:, cs], preferred_element_type=F32))
        mixed = jnp.concatenate(parts, axis=-1) + bias_ref[0]
        o_ref[rs, :] = (gu_ref[rs, :].astype(F32) * mixed).astype(o_ref.dtype)


def gmlp_mix(gu_act, vn, wmix, bias, *, tm, prompt_rows):
    m = gu_act.shape[0]
    first_sample_tile = prompt_rows // tm
    kind = lambda i: jnp.where(i >= first_sample_tile, 1, 0)
    return pl.pallas_call(
        functools.partial(_gmlp_kernel, n_chunks=tm // GM_CHUNK),
        grid=(m // tm,),
        in_specs=[
            pl.BlockSpec((tm, GM_WIDTH), lambda i: (i, 0)),
            pl.BlockSpec((tm, GM_WIDTH), lambda i: (i, 0)),
            pl.BlockSpec((1, GM_GROUPS, GM_CHUNK, GM_CHUNK), lambda i: (kind(i), 0, 0, 0)),
            pl.BlockSpec((1, GM_CHUNK, GM_WIDTH), lambda i: (kind(i), 0, 0)),
        ],
        out_specs=pl.BlockSpec((tm, GM_WIDTH), lambda i: (i, 0)),
        out_shape=jax.ShapeDtypeStruct((m, GM_WIDTH), BF16),
        compiler_params=pltpu.CompilerParams(dimension_semantics=("arbitrary",), vmem_limit_bytes=32 << 20),
        name="gmlp_mix",
    )(gu_act, vn, wmix, bias)


def _rope_tile(t, cos_t, sin_t):
    return t * cos_t + pltpu.roll(t, QK_ROPE, 1) * sin_t


def _epi_ffn_up(accs, row_aux, col_aux):
    return [jax.nn.silu(accs[0]) * accs[1]]


def _epi_small(accs, row_aux, col_aux):
    y = accs[0]
    cos_t, sin_t = row_aux
    qg, kvg = col_aux
    cqn = _rms(y[:, :Q_LORA], qg)
    ckvn = _rms(y[:, Q_LORA:Q_LORA + KV_LORA], kvg)
    roped = _rope_tile(y[:, Q_LORA + KV_LORA:], cos_t, sin_t)
    return [cqn, jnp.concatenate([ckvn, roped[:, :QK_ROPE]], axis=-1)]


def _epi_identity(accs, row_aux, col_aux):
    return [accs[0]]


def _epi_gelu(accs, row_aux, col_aux):
    return [jax.nn.gelu(accs[0])]


def _epi_gelu_ln(accs, row_aux, col_aux):
    y = jax.nn.gelu(accs[0])
    g, b = col_aux
    mu = jnp.mean(y, axis=-1, keepdims=True)
    var = jnp.mean(jnp.square(y - mu), axis=-1, keepdims=True)
    return [(y - mu) * lax.rsqrt(var + EPS) * g + b]


def _epi_sigmoid(accs, row_aux, col_aux):
    return [jax.nn.sigmoid(accs[0])]


def _epi_q_rope(accs, row_aux, col_aux):
    y = accs[0]
    cos_t, sin_t = row_aux
    parts = []
    for h in range(MLA_HEADS):
        base = h * Q_HEAD_PAD
        parts.append(y[:, base:base + QK_NOPE])
        parts.append(_rope_tile(y[:, base + QK_NOPE:base + Q_HEAD_PAD], cos_t, sin_t))
    return [jnp.concatenate(parts, axis=-1)]


def _epi_kv_split(accs, row_aux, col_aux):
    n_k = MLA_HEADS * Q_HEAD_PAD
    return [accs[0][:, :n_k], accs[0][:, n_k:]]


def _rot_cols(w):
    half = QK_ROPE // 2
    return jnp.concatenate([-w[..., half:], w[..., :half]], axis=-1)


def _prep_layer(l, w_in, w_uq, w_ukv, gmlp_w_s, gmlp_b_s, t_sample):
    wi = w_in[l]
    o_kr = Q_LORA + KV_LORA
    o_h = o_kr + QK_ROPE
    w_small = jnp.concatenate([wi[:, :o_h], _rot_cols(wi[:, o_kr:o_h])], axis=-1).astype(BF16)
    w_hgrn = wi[:, o_h:o_h + 4 * HG_WIDTH].astype(BF16)
    o_g = o_h + 4 * HG_WIDTH
    w_gu = wi[:, o_g:o_g + GM_WIDTH].astype(BF16)
    w_gv = wi[:, o_g + GM_WIDTH:o_g + 2 * GM_WIDTH].astype(BF16)
    w_gate = wi[:, o_g + 2 * GM_WIDTH:].astype(BF16)

    uq = w_uq[l].reshape(Q_LORA, MLA_HEADS, QK_NOPE + QK_ROPE)
    uq_rope = uq[..., QK_NOPE:]
    w_q = jnp.concatenate([uq[..., :QK_NOPE], uq_rope, _rot_cols(uq_rope)], axis=-1)
    w_q = w_q.reshape(Q_LORA, MLA_HEADS * Q_HEAD_PAD).astype(BF16)

    ukv = w_ukv[l].reshape(KV_LORA, MLA_HEADS, QK_NOPE + V_DIM)
    w_uk, w_uv = ukv[..., :QK_NOPE], ukv[..., QK_NOPE:]
    k_top = jnp.concatenate([w_uk, jnp.zeros((KV_LORA, MLA_HEADS, Q_HEAD_PAD - QK_NOPE), F32)], axis=-1)
    eye = jnp.concatenate([jnp.zeros((QK_ROPE, QK_NOPE), F32), jnp.eye(QK_ROPE, dtype=F32),
                           jnp.zeros((QK_ROPE, Q_HEAD_PAD - QK_NOPE - QK_ROPE), F32)], axis=-1)
    k_bot = jnp.broadcast_to(eye[:, None, :], (QK_ROPE, MLA_HEADS, Q_HEAD_PAD))
    w_k = jnp.concatenate([k_top, k_bot], axis=0).reshape(CACHE_DIM, MLA_HEADS * Q_HEAD_PAD).astype(BF16)
    w_v = jnp.concatenate([w_uv, jnp.zeros((QK_ROPE, MLA_HEADS, V_DIM), F32)], axis=0)
    w_v = w_v.reshape(CACHE_DIM, MLA_WIDTH).astype(BF16)
    w_lat = jnp.transpose(w_uk, (1, 2, 0)).astype(BF16)
    w_uvh = jnp.transpose(w_uv, (1, 0, 2)).astype(BF16)

    ws = gmlp_w_s[l]
    w_prompt = jnp.tril(ws)
    small = jnp.tril(ws[:, :t_sample, :t_sample])
    reps = GM_CHUNK // t_sample
    w_sample = jnp.einsum("ab,gts->gatbs", jnp.eye(reps, dtype=F32), small).reshape(GM_GROUPS, GM_CHUNK, GM_CHUNK)
    wmix = jnp.stack([w_prompt, w_sample]).astype(BF16)
    bs = gmlp_b_s[l]
    bias_p = jnp.repeat(bs.T[:, :, None], GM_GROUP_DIM, axis=2).reshape(GM_CHUNK, GM_WIDTH)
    bias_s = jnp.tile(jnp.repeat(bs[:, :t_sample].T[:, :, None], GM_GROUP_DIM, axis=2).reshape(t_sample, GM_WIDTH),
                      (reps, 1))
    bias = jnp.stack([bias_p, bias_s])
    return dict(w_small=w_small, w_hgrn=w_hgrn, w_gu=w_gu, w_gv=w_gv, w_gate=w_gate, w_q=w_q,
                w_kv=jnp.concatenate([w_k, w_v], axis=-1), w_lat=w_lat, w_uvh=w_uvh, wmix=wmix, bias=bias)


def _rope_tables(pos):
    half = QK_ROPE // 2
    inv = ROPE_THETA ** (-jnp.arange(half, dtype=F32) / half)
    ang = pos.astype(F32)[:, None] * inv[None, :]
    pad = jnp.zeros((pos.shape[0], LANES - QK_ROPE), F32)
    cos_t = jnp.concatenate([jnp.cos(ang), jnp.cos(ang), pad], axis=-1)
    sin_t = jnp.concatenate([jnp.sin(ang), jnp.sin(ang), pad], axis=-1)
    return cos_t, sin_t


def kernel(x_prompt, x_sample, cache_mla, state_hgrn, page_table, norm_gains, w_ffn1_gate, w_ffn1_up, w_ffn1_down, w_ffn2_gate, w_ffn2_up, w_ffn2_down, w_in, mla_q_norm, mla_kv_norm, w_uq, w_ukv, hgrn_lb_logits, hgrn_out_norm, gmlp_ln_g, gmlp_ln_b, gmlp_w_s, gmlp_b_s, w_br_mla, w_br_hgrn, w_br_gmlp, w_out):
    batch, seq, d = x_prompt.shape
    nb, t_s, _ = x_sample.shape
    m_p, m_s = batch * seq, nb * t_s
    m = m_p + m_s
    past = page_table.shape[1] * PAGE_SIZE
    tm = 1024
    while m_p % tm or m_s % tm:
        tm //= 2
    th = min(512, tm)
    tq4 = min(256, tm)
    assert tm >= GM_CHUNK and m_p % m_s == 0

    pos = jnp.concatenate([jnp.tile(jnp.arange(seq), batch), jnp.tile(past + jnp.arange(t_s), nb)])
    cos_t, sin_t = _rope_tables(pos)
    bf = lambda w: w.astype(BF16)
    cache_t = jnp.swapaxes(cache_mla, 2, 3)

    rows_all, s_p_all, s_s_all, vn_all = [], [], [], []
    x, xn = join_rmsnorm(x_prompt.reshape(m_p, d), x_sample.reshape(m_s, d), norm_gains[0, 0], tm=th)
    for l in range(DEPTH):
        ng = norm_gains[l]
        p = _prep_layer(l, w_in, w_uq, w_ukv, gmlp_w_s, gmlp_b_s, t_s)

        hmid = mm(xn, [w_ffn1_gate, w_ffn1_up], _epi_ffn_up, [(512, BF16)], tm=tm, tn=512, layer=l,
                  name="ffn1_up")[0]
        x, hn = mm_res(hmid, bf(w_ffn1_down[l]), x, ng[1], ng[2], alpha=0.5, tm=tq4, name="ffn1_down")

        cqn, rows = mm(hn, [p["w_small"]], _epi_small, [(Q_LORA, BF16), (CACHE_DIM, F32)], tm=th,
                       tn=p["w_small"].shape[1], row_aux=(cos_t, sin_t),
                       col_aux=(mla_q_norm[l].reshape(1, -1), mla_kv_norm[l].reshape(1, -1)), name="in_proj_mla")
        hproj = mm(hn, [p["w_hgrn"]], _epi_identity, [(1024, F32)], tm=tm, tn=1024, name="in_proj_hgrn")[0]
        gu_act = mm(hn, [p["w_gu"]], _epi_gelu, [(GM_WIDTH, BF16)], tm=tm, tn=GM_WIDTH, name="in_proj_gu")[0]
        vn = mm(hn, [p["w_gv"]], _epi_gelu_ln, [(GM_WIDTH, F32)], tm=tm, tn=GM_WIDTH,
                col_aux=(gmlp_ln_g[l].reshape(1, -1), gmlp_ln_b[l].reshape(1, -1)), name="in_proj_gv")[0]
        gate = mm(hn, [p["w_gate"]], _epi_sigmoid, [(1024, BF16)], tm=tm, tn=1024, name="in_proj_gate")[0]

        q = mm(cqn, [p["w_q"]], _epi_q_rope, [(MLA_HEADS * Q_HEAD_PAD, BF16)], tm=th, tn=MLA_HEADS * Q_HEAD_PAD,
               row_aux=(cos_t, sin_t), name="mla_q")[0]
        kfull, vv = mm(rows, [p["w_kv"]], _epi_kv_split, [(MLA_HEADS * Q_HEAD_PAD, BF16), (MLA_WIDTH, BF16)],
                       tm=th, tn=p["w_kv"].shape[1], rows=m_p, name="mla_kv")
        a_p = mla_prompt_attention(q, kfull, vv, batch=batch, seq=seq, tq=min(1024, seq), hpb=2)

        qcat = sample_query_latent(q, p["w_lat"], row0=m_p, nb=nb, t=t_s)
        o_lat = mla_sample_attention(qcat.reshape(nb, MLA_HEADS * t_s, CACHE_DIM), rows[m_p:].reshape(nb, t_s, CACHE_DIM),
                                     cache_t, page_table, layer=l, group=32)
        a_s = sample_value_proj(o_lat.reshape(nb, MLA_HEADS, t_s, KV_LORA), p["w_uvh"], nb=nb, t=t_s)

        b_p, s_p = hgrn_mixer(hproj, hgrn_lb_logits, hgrn_out_norm[l], None, layer=l, row0=0, nseq_total=batch,
                              seq_len=seq, blk_len=256, nseq=1, sub=HGRN_SUB, wide=HGRN_WIDE, hpb=4)
        b_s, s_s = hgrn_mixer(hproj, hgrn_lb_logits, hgrn_out_norm[l], state_hgrn, layer=l, row0=m_p,
                              nseq_total=nb, seq_len=t_s, blk_len=t_s, nseq=8, sub=t_s, hpb=4)

        cmix = gmlp_mix(gu_act, vn, p["wmix"], p["bias"], tm=th, prompt_rows=m_p)

        merged = merge_branches(a_p, a_s, b_p, b_s, cmix, gate, bf(w_br_mla[l]), bf(w_br_hgrn[l]),
                                bf(w_br_gmlp[l]), tm=th)
        x, hn4 = mm_res(merged, bf(w_out[l]), x, ng[3], ng[4], alpha=1.0, tm=th, name="merge_out")

        hmid = mm(hn4, [w_ffn2_gate, w_ffn2_up], _epi_ffn_up, [(512, BF16)], tm=tm, tn=512, layer=l,
                  name="ffn2_up")[0]
        g_next = norm_gains[l + 1, 0] if l + 1 < DEPTH else None
        x, xn = mm_res(hmid, bf(w_ffn2_down[l]), x, ng[5], g_next, alpha=0.5, tm=tq4, name="ffn2_down")

        rows_all.append(rows)
        s_p_all.append(s_p)
        s_s_all.append(s_s)
        vn_all.append(vn[m_p:])

    rows_st = jnp.stack(rows_all)
    return (x[:m_p].reshape(batch, seq, d), x[m_p:].reshape(nb, t_s, d),
            rows_st[:, :m_p].reshape(DEPTH, batch, seq, CACHE_DIM), rows_st[:, m_p:].reshape(DEPTH, nb, t_s, CACHE_DIM),
            jnp.stack(s_p_all), jnp.stack(s_s_all), jnp.stack(vn_all).reshape(DEPTH, nb, t_s, GM_WIDTH))
```
